```python
import math
import jax, jax.numpy as jnp
from jax import lax
import numpy as np

D_MODEL = 2048
BATCH = 2
SEQ = 4096
DEPTH = 1
DEC_BATCH = 32
DEC_SEQ = 4
PAST_LEN = 8192
PAGE_SIZE = 128

N_HEADS = 8
HEAD_DK = 64
HEAD_DV = 2 * HEAD_DK
QK_COLS = N_HEADS * 2 * HEAD_DK
ATTN_WIDTH = N_HEADS * HEAD_DV
ROPE_DIMS = HEAD_DK // 4
ROPE_THETA = 500000.0
Q_BLOCK = 128
NEG_INF = -1e30
SSM_WIDTH = D_MODEL // 2
SSM_GROUP = 16
SSM_GROUPS = SSM_WIDTH // SSM_GROUP
SSM_STATE = 64
DT_MIN = 1e-3
DT_MAX = 1e-1
PEER_HEADS = 8
PEER_KEYS = 128
PEER_EXPERTS = PEER_KEYS * PEER_KEYS
PEER_DKEY = 256
PEER_TOPK = 16
PEER_BLOCK = 128
PLE_DIM = 256
RMS_EPS = 1e-6
V_END = 2 * QK_COLS + ATTN_WIDTH
U_END = V_END + SSM_WIDTH
IN_COLS = U_END + 2 * D_MODEL

kernel_name = 'diffattn_s5_peer_hybrid_step'


def rmsnorm(x, g):
    xf = x.astype(jnp.float32)
    y = xf * lax.rsqrt(jnp.mean(xf * xf, axis=-1, keepdims=True) + RMS_EPS)
    return y.astype(x.dtype) * g


def partial_rope(x, pos):
    half = ROPE_DIMS // 2
    inv_freq = ROPE_THETA ** (-jnp.arange(half, dtype=jnp.float32) / half)
    ang = pos.astype(jnp.float32)[:, None] * inv_freq[None, :]
    cos = jnp.cos(ang)[None, :, None, None, :]
    sin = jnp.sin(ang)[None, :, None, None, :]
    xf = x.astype(jnp.float32)
    x1, x2 = xf[..., :half], xf[..., half:ROPE_DIMS]
    out = jnp.concatenate([x1 * cos - x2 * sin, x2 * cos + x1 * sin, xf[..., ROPE_DIMS:]], axis=-1)
    return out.astype(x.dtype)


def diff_attention(q, q_pos, segments, lam):
    B, Tq = q.shape[0], q.shape[1]
    qb = min(Q_BLOCK, Tq)
    n_blk = -(-Tq // qb)
    pad = n_blk * qb - Tq
    q = jnp.pad(q, ((0, 0), (0, pad), (0, 0), (0, 0), (0, 0)))
    q_pos = jnp.pad(q_pos, (0, pad), mode='edge')
    q_blocks = jnp.moveaxis(q.reshape(B, n_blk, qb, N_HEADS, 2, HEAD_DK), 1, 0)
    p_blocks = q_pos.reshape(n_blk, qb)
    scale = HEAD_DK ** -0.5

    def one_block(args):
        q_blk, pos_blk = args
        scores = []
        for k, _, k_pos in segments:
            s = jnp.einsum('bqhmd,bkhmd->bhmqk', q_blk, k).astype(jnp.float32) * scale
            scores.append(jnp.where(k_pos[None, :] <= pos_blk[:, None], s, NEG_INF))
        probs = jax.nn.softmax(jnp.concatenate(scores, axis=-1), axis=-1)
        wts = probs[:, :, 0] - lam * probs[:, :, 1]
        outs = []
        off = 0
        for k, v, _ in segments:
            tk = k.shape[1]
            outs.append(jnp.einsum('bhqk,bkhd->bqhd', wts[..., off:off + tk].astype(v.dtype), v))
            off += tk
        return sum(outs)

    out = lax.map(one_block, (q_blocks, p_blocks))
    return jnp.moveaxis(out, 0, 1).reshape(B, n_blk * qb, N_HEADS, HEAD_DV)[:, :Tq]


def _ssm_combine(e1, e2):
    a1r, a1i, b1r, b1i = e1
    a2r, a2i, b2r, b2i = e2
    return (a1r * a2r - a1i * a2i,
            a1r * a2i + a1i * a2r,
            a2r * b1r - a2i * b1i + b2r,
            a2r * b1i + a2i * b1r + b2i)


def s5_scan(u, h0, a_re, a_im, log_dt, b_re, b_im, c_re, c_im, d_skip):
    f32 = jnp.float32
    uf = u.astype(f32)
    a_re, a_im = a_re.astype(f32), a_im.astype(f32)
    dt = jnp.exp(log_dt.astype(f32))[:, None]
    mag = jnp.exp(a_re * dt)
    ab_re, ab_im = mag * jnp.cos(a_im * dt), mag * jnp.sin(a_im * dt)
    den = a_re * a_re + a_im * a_im
    f_re = ((ab_re - 1.0) * a_re + ab_im * a_im) / den
    f_im = (ab_im * a_re - (ab_re - 1.0) * a_im) / den
    b_re, b_im = b_re.astype(f32), b_im.astype(f32)
    bb_re = f_re[..., None] * b_re - f_im[..., None] * b_im
    bb_im = f_re[..., None] * b_im + f_im[..., None] * b_re
    bu_re = jnp.einsum('gpc,btgc->btgp', bb_re, uf)
    bu_im = jnp.einsum('gpc,btgc->btgp', bb_im, uf)
    elems = (jnp.broadcast_to(ab_re, bu_re.shape), jnp.broadcast_to(ab_im, bu_im.shape), bu_re, bu_im)
    a_cum_re, a_cum_im, h_re, h_im = lax.associative_scan(_ssm_combine, elems, axis=1)
    if h0 is not None:
        h0_re = h0[0].astype(f32)[:, None]
        h0_im = h0[1].astype(f32)[:, None]
        h_re, h_im = (h_re + a_cum_re * h0_re - a_cum_im * h0_im,
                      h_im + a_cum_re * h0_im + a_cum_im * h0_re)
    y = (jnp.einsum('btgp,gcp->btgc', h_re, c_re.astype(f32))
         - jnp.einsum('btgp,gcp->btgc', h_im, c_im.astype(f32))
         + d_skip.astype(f32) * uf)
    return y.astype(u.dtype), h_re[:, -1], h_im[:, -1]


def peer_ffn(h, w_query, sub_keys, expert_u, expert_v):
    B, T, D = h.shape
    n = B * T
    xt = h.reshape(n, D)
    q = (xt @ w_query).reshape(n, PEER_HEADS, 2, PEER_DKEY // 2)
    s = jnp.einsum('nhmd,mkd->nhmk', q, sub_keys).astype(jnp.float32)
    sv, si = lax.top_k(s, PEER_TOPK)
    n_cand = PEER_TOPK * PEER_TOPK
    cand = (sv[:, :, 0, :, None] + sv[:, :, 1, None, :]).reshape(n, PEER_HEADS, n_cand)
    cidx = (si[:, :, 0, :, None] * PEER_KEYS + si[:, :, 1, None, :]).reshape(n, PEER_HEADS, n_cand)
    fv, fpos = lax.top_k(cand, PEER_TOPK)
    eidx = jnp.take_along_axis(cidx, fpos, axis=-1)
    gate = jax.nn.softmax(fv, axis=-1).astype(h.dtype)
    blk = min(PEER_BLOCK, n)
    n_blk = -(-n // blk)
    pad = n_blk * blk - n
    xb = jnp.pad(xt, ((0, pad), (0, 0))).reshape(n_blk, blk, D)
    ib = jnp.pad(eidx, ((0, pad), (0, 0), (0, 0))).reshape(n_blk, blk, PEER_HEADS, PEER_TOPK)
    gb = jnp.pad(gate, ((0, pad), (0, 0), (0, 0))).reshape(n_blk, blk, PEER_HEADS, PEER_TOPK)

    def one_block(args):
        x_blk, i_blk, g_blk = args
        act = jax.nn.gelu(jnp.einsum('nd,nhkd->nhk', x_blk, expert_u[i_blk]))
        return jnp.einsum('nhk,nhkd->nd', g_blk * act, expert_v[i_blk])

    out = lax.map(one_block, (xb, ib, gb)).reshape(n_blk * blk, D)[:n]
    return out.reshape(B, T, D)


def decoder_layer(x, p, pos, past, lam_init, g_mix, w_in, g_q, g_k, lambda_q, lambda_k, g_head, w_attn_up,
                  ssm_a_re, ssm_a_im, ssm_log_dt, ssm_b_re, ssm_b_im, ssm_c_re, ssm_c_im, ssm_d,
                  w_glu, w_out, g_ffn, peer_w_query, peer_sub_keys, peer_u, peer_v, g_ple, w_ple, w_ple_gate):
    B, T, _ = x.shape
    h = rmsnorm(x, g_mix)
    z = h @ w_in
    q = z[..., :QK_COLS].reshape(B, T, N_HEADS, 2, HEAD_DK)
    k = z[..., QK_COLS:2 * QK_COLS].reshape(B, T, N_HEADS, 2, HEAD_DK)
    v = z[..., 2 * QK_COLS:V_END].reshape(B, T, N_HEADS, HEAD_DV)
    u = z[..., V_END:U_END].reshape(B, T, SSM_GROUPS, SSM_GROUP)
    gate_a = jax.nn.sigmoid(z[..., U_END:U_END + D_MODEL])
    gate_s = jax.nn.sigmoid(z[..., U_END + D_MODEL:])

    q = partial_rope(rmsnorm(q, g_q), pos)
    k = partial_rope(rmsnorm(k, g_k), pos)
    lq = lambda_q.astype(jnp.float32)
    lk = lambda_k.astype(jnp.float32)
    lam = jnp.exp(jnp.sum(lq[0] * lk[0])) - jnp.exp(jnp.sum(lq[1] * lk[1])) + lam_init
    segments = [(k, v, pos)]
    h0 = None
    if past is not None:
        k_past, v_past, h0 = past
        past_pos = jnp.arange(k_past.shape[1], dtype=jnp.int32)
        segments = [(k_past, v_past, past_pos), (k, v, pos)]
    att = diff_attention(q, pos, segments, lam)
    att = rmsnorm(att, g_head) * (1.0 - lam_init)
    branch_a = att.reshape(B, T, ATTN_WIDTH) @ w_attn_up

    y_ssm, h_re, h_im = s5_scan(u, h0, ssm_a_re, ssm_a_im, ssm_log_dt, ssm_b_re, ssm_b_im,
                                ssm_c_re, ssm_c_im, ssm_d)
    glu = jax.nn.gelu(y_ssm.reshape(B, T, SSM_WIDTH)) @ w_glu
    branch_s = glu[..., :D_MODEL] * jax.nn.sigmoid(glu[..., D_MODEL:])

    x = x + (gate_a * branch_a + gate_s * branch_s) @ w_out
    x = x + peer_ffn(rmsnorm(x, g_ffn), peer_w_query, peer_sub_keys, peer_u, peer_v)
    x = x + (p @ w_ple) * jax.nn.sigmoid(rmsnorm(x, g_ple) @ w_ple_gate)
    return x, k.reshape(B, T, N_HEADS, 2 * HEAD_DK), v, h_re, h_im


def setup_inputs(seed: int = 0) -> dict:
    key = jax.random.key(seed)
    keys = iter(jax.random.split(key, 48))

    def nrm(shape, scale):
        return jax.random.normal(next(keys), shape, jnp.float32) * scale

    def gain(shape):
        return 1.0 + nrm(shape, 0.05)

    L = DEPTH
    n_pages = PAST_LEN // PAGE_SIZE
    n_used = DEC_BATCH * n_pages
    n_pool = n_used + max(1, n_used // 4)
    page_table = jax.random.permutation(next(keys), n_pool)[:n_used].reshape(DEC_BATCH, n_pages).astype(jnp.int32)
    a_im_base = jnp.pi * jnp.arange(SSM_STATE, dtype=jnp.float32)
    return {
        'x_prompt': nrm((BATCH, SEQ, D_MODEL), 1.0),
        'x_sample': nrm((DEC_BATCH, DEC_SEQ, D_MODEL), 1.0),
        'cache_k': nrm((L, n_pool, PAGE_SIZE, N_HEADS, 2 * HEAD_DK), 1.0),
        'cache_v': nrm((L, n_pool, PAGE_SIZE, N_HEADS, HEAD_DV), 1.0),
        'state_ssm_re': nrm((L, DEC_BATCH, SSM_GROUPS, SSM_STATE), 0.1),
        'state_ssm_im': nrm((L, DEC_BATCH, SSM_GROUPS, SSM_STATE), 0.1),
        'page_table': page_table,
        'p_prompt': nrm((L, BATCH, SEQ, PLE_DIM), 1.0),
        'p_sample': nrm((L, DEC_BATCH, DEC_SEQ, PLE_DIM), 1.0),
        'g_mix': gain((L, D_MODEL)),
        'w_in': nrm((L, D_MODEL, IN_COLS), D_MODEL ** -0.5),
        'g_q': gain((L, 2, HEAD_DK)),
        'g_k': gain((L, 2, HEAD_DK)),
        'lambda_q': nrm((L, 2, HEAD_DK), 0.1),
        'lambda_k': nrm((L, 2, HEAD_DK), 0.1),
        'g_head': gain((L, HEAD_DV)),
        'w_attn_up': nrm((L, ATTN_WIDTH, D_MODEL), ATTN_WIDTH ** -0.5),
        'ssm_a_re': -0.5 + nrm((L, SSM_GROUPS, SSM_STATE), 0.01),
        'ssm_a_im': a_im_base + nrm((L, SSM_GROUPS, SSM_STATE), 0.01),
        'ssm_log_dt': jax.random.uniform(next(keys), (L, SSM_GROUPS), jnp.float32,
                                         math.log(DT_MIN), math.log(DT_MAX)),
        'ssm_b_re': nrm((L, SSM_GROUPS, SSM_STATE, SSM_GROUP), (2 * SSM_GROUP) ** -0.5),
        'ssm_b_im': nrm((L, SSM_GROUPS, SSM_STATE, SSM_GROUP), (2 * SSM_GROUP) ** -0.5),
        'ssm_c_re': nrm((L, SSM_GROUPS, SSM_GROUP, SSM_STATE), SSM_STATE ** -0.5),
        'ssm_c_im': nrm((L, SSM_GROUPS, SSM_GROUP, SSM_STATE), SSM_STATE ** -0.5),
        'ssm_d': nrm((L, SSM_GROUPS, SSM_GROUP), 1.0),
        'w_glu': nrm((L, SSM_WIDTH, 2 * D_MODEL), SSM_WIDTH ** -0.5),
        'w_out': nrm((L, D_MODEL, D_MODEL), D_MODEL ** -0.5),
        'g_ffn': gain((L, D_MODEL)),
        'peer_w_query': nrm((L, D_MODEL, PEER_HEADS * PEER_DKEY), D_MODEL ** -0.5),
        'peer_sub_keys': nrm((L, 2, PEER_KEYS, PEER_DKEY // 2), (PEER_DKEY // 2) ** -0.5),
        'peer_u': nrm((L, PEER_EXPERTS, D_MODEL), D_MODEL ** -0.5),
        'peer_v': nrm((L, PEER_EXPERTS, D_MODEL), PEER_TOPK ** -0.5),
        'g_ple': gain((L, D_MODEL)),
        'w_ple': nrm((L, PLE_DIM, D_MODEL), PLE_DIM ** -0.5),
        'w_ple_gate': nrm((L, D_MODEL, D_MODEL), D_MODEL ** -0.5),
    }


def reference(x_prompt, x_sample, cache_k, cache_v, state_ssm_re, state_ssm_im, page_table, p_prompt, p_sample,
              g_mix, w_in, g_q, g_k, lambda_q, lambda_k, g_head, w_attn_up,
              ssm_a_re, ssm_a_im, ssm_log_dt, ssm_b_re, ssm_b_im, ssm_c_re, ssm_c_im, ssm_d,
              w_glu, w_out, g_ffn, peer_w_query, peer_sub_keys, peer_u, peer_v, g_ple, w_ple, w_ple_gate):
    n_dec = x_sample.shape[0]
    past_len = page_table.shape[1] * PAGE_SIZE
    pos_prompt = jnp.arange(x_prompt.shape[1], dtype=jnp.int32)
    pos_sample = past_len + jnp.arange(x_sample.shape[1], dtype=jnp.int32)
    layer_weights = (g_mix, w_in, g_q, g_k, lambda_q, lambda_k, g_head, w_attn_up,
                     ssm_a_re, ssm_a_im, ssm_log_dt, ssm_b_re, ssm_b_im, ssm_c_re, ssm_c_im, ssm_d,
                     w_glu, w_out, g_ffn, peer_w_query, peer_sub_keys, peer_u, peer_v, g_ple, w_ple, w_ple_gate)
    y_prompt, y_sample = x_prompt, x_sample
    kp, vp, srp, sip, ks, vs, srs, sis = [], [], [], [], [], [], [], []
    for i in range(DEPTH):
        lw = [w[i] for w in layer_weights]
        lam_init = 0.8 - 0.6 * math.exp(-0.3 * i)
        y_prompt, k_new, v_new, h_re, h_im = decoder_layer(y_prompt, p_prompt[i], pos_prompt, None, lam_init, *lw)
        kp.append(k_new); vp.append(v_new); srp.append(h_re); sip.append(h_im)
        k_past = cache_k[i, page_table].reshape(n_dec, past_len, N_HEADS, 2, HEAD_DK)
        v_past = cache_v[i, page_table].reshape(n_dec, past_len, N_HEADS, HEAD_DV)
        past = (k_past, v_past, (state_ssm_re[i], state_ssm_im[i]))
        y_sample, k_new, v_new, h_re, h_im = decoder_layer(y_sample, p_sample[i], pos_sample, past, lam_init, *lw)
        ks.append(k_new); vs.append(v_new); srs.append(h_re); sis.append(h_im)
    return (y_prompt, y_sample,
            jnp.stack(kp), jnp.stack(vp), jnp.stack(srp), jnp.stack(sip),
            jnp.stack(ks), jnp.stack(vs), jnp.stack(srs), jnp.stack(sis))
```

```python
import functools
import math

import jax
import jax.numpy as jnp
from jax import lax
from jax.experimental import pallas as pl
from jax.experimental.pallas import tpu as pltpu

F32 = jnp.float32
BF16 = jnp.bfloat16

D_MODEL = 2048
PAGE_SIZE = 128
N_HEADS = 8
HEAD_DK = 64
HEAD_DV = 2 * HEAD_DK
QK_COLS = N_HEADS * 2 * HEAD_DK
ATTN_WIDTH = N_HEADS * HEAD_DV
ROPE_DIMS = HEAD_DK // 4
ROPE_THETA = 500000.0
NEG_INF = -1e30
SSM_WIDTH = D_MODEL // 2
SSM_GROUP = 16
SSM_GROUPS = SSM_WIDTH // SSM_GROUP
SSM_STATE = 64
PEER_HEADS = 8
PEER_KEYS = 128
PEER_EXPERTS = PEER_KEYS * PEER_KEYS
PEER_DKEY = 256
PEER_TOPK = 16
PLE_DIM = 256
RMS_EPS = 1e-6
V_END = 2 * QK_COLS + ATTN_WIDTH
U_END = V_END + SSM_WIDTH
IN_COLS = U_END + 2 * D_MODEL

LANES = 128
SUBLANES = 8
VMEM_LIMIT = 52 * 1024 * 1024
SSM_GB = LANES // SSM_GROUP
SSM_NB = SSM_GROUPS // SSM_GB
SSM_SW = SSM_GB * SSM_STATE
SSM_CHUNK = 128
NEW_ROWS = 2 * SUBLANES


def _params(sem):
    return pltpu.CompilerParams(dimension_semantics=sem, vmem_limit_bytes=VMEM_LIMIT)


def _nt_dot(a, b):
    return lax.dot_general(a, b, (((1,), (1,)), ((), ())), preferred_element_type=F32)


def _in_proj_kernel(x_ref, g_ref, w_ref, o_ref, h_ref):
    @pl.when(pl.program_id(1) == 0)
    def _():
        x = x_ref[...]
        ms = jnp.mean(x * x, axis=-1, keepdims=True)
        h_ref[...] = (x * lax.rsqrt(ms + RMS_EPS) * g_ref[...]).astype(BF16)

    o_ref[...] = jnp.dot(h_ref[...], w_ref[...], preferred_element_type=F32)


def _in_proj(x, g, w_bf, tm, tn):
    n, d = x.shape
    cols = w_bf.shape[1]
    return pl.pallas_call(
        _in_proj_kernel,
        grid=(n // tm, cols // tn),
        in_specs=[
            pl.BlockSpec((tm, d), lambda i, j: (i, 0)),
            pl.BlockSpec((1, d), lambda i, j: (0, 0)),
            pl.BlockSpec((d, tn), lambda i, j: (0, j)),
        ],
        out_specs=pl.BlockSpec((tm, tn), lambda i, j: (i, j)),
        out_shape=jax.ShapeDtypeStruct((n, cols), F32),
        scratch_shapes=[pltpu.VMEM((tm, d), BF16)],
        compiler_params=_params(("parallel", "arbitrary")),
        name="in_proj",
    )(x, g, w_bf)


def _qkv_prep_kernel(zq_ref, zk_ref, zv_ref, gq_ref, gk_ref, bd_ref, c_ref, s1_ref, s2_ref,
                     q_ref, kf_ref, kb_ref, vb_ref):
    bd = bd_ref[...]
    cos, sin_lo, sin_hi = c_ref[...], s1_ref[...], s2_ref[...]

    def norm_rope(x, g):
        x2 = x * x
        hi = x2.astype(BF16)
        lo = (x2 - hi.astype(F32)).astype(BF16)
        ss = (jnp.dot(hi, bd, preferred_element_type=F32)
              + jnp.dot(lo, bd, preferred_element_type=F32))
        xn = x * lax.rsqrt(ss * (1.0 / HEAD_DK) + RMS_EPS) * g
        fwd = pltpu.roll(xn, LANES - ROPE_DIMS // 2, axis=1)
        bwd = pltpu.roll(xn, ROPE_DIMS // 2, axis=1)
        return xn * cos + fwd * sin_lo + bwd * sin_hi

    for c in range(QK_COLS // LANES):
        sl = slice(c * LANES, (c + 1) * LANES)
        q = norm_rope(zq_ref[:, sl], gq_ref[...])
        q_ref[:, sl] = (q * (HEAD_DK ** -0.5)).astype(BF16)
        k = norm_rope(zk_ref[:, sl], gk_ref[...])
        kf_ref[:, sl] = k
        kb_ref[:, sl] = k.astype(BF16)
    vb_ref[...] = zv_ref[...].astype(BF16)


def _qkv_prep(z, gq, gk, bd, cos, sin_lo, sin_hi, tm):
    n = z.shape[0]
    nt = cos.shape[0] // tm
    row = lambda c: pl.BlockSpec((tm, QK_COLS), lambda i: (i, c))
    vec = pl.BlockSpec((1, LANES), lambda i: (0, 0))
    tab = pl.BlockSpec((tm, LANES), lambda i: (i % nt, 0))
    out = pl.BlockSpec((tm, QK_COLS), lambda i: (i, 0))
    return pl.pallas_call(
        _qkv_prep_kernel,
        grid=(n // tm,),
        in_specs=[row(0), row(1), row(2), vec, vec,
                  pl.BlockSpec((LANES, LANES), lambda i: (0, 0)), tab, tab, tab],
        out_specs=[out, out, out, out],
        out_shape=[jax.ShapeDtypeStruct((n, QK_COLS), BF16),
                   jax.ShapeDtypeStruct((n, QK_COLS), F32),
                   jax.ShapeDtypeStruct((n, QK_COLS), BF16),
                   jax.ShapeDtypeStruct((n, ATTN_WIDTH), BF16)],
        compiler_params=_params(("parallel",)),
        name="qkv_prep",
    )(z, z, z, gq, gk, bd, cos, sin_lo, sin_hi)


def _rope_tables(pos):
    half = ROPE_DIMS // 2
    inv_freq = ROPE_THETA ** (-jnp.arange(half, dtype=F32) / half)
    ang = pos.astype(F32)[:, None] * inv_freq[None, :]
    cos, sin = jnp.cos(ang), jnp.sin(ang)
    ones = jnp.ones((pos.shape[0], HEAD_DK - ROPE_DIMS), F32)
    zeros8 = jnp.zeros_like(sin)
    zeros = jnp.zeros_like(ones)
    c = jnp.concatenate([cos, cos, ones], axis=1)
    s_lo = jnp.concatenate([-sin, zeros8, zeros], axis=1)
    s_hi = jnp.concatenate([zeros8, sin, zeros], axis=1)
    two = lambda a: jnp.concatenate([a, a], axis=1)
    return two(c), two(s_lo), two(s_hi)


def _softmax_update(s, v, m_ref, l_ref, acc_ref):
    m_prev = m_ref[...]
    m_new = jnp.maximum(m_prev, jnp.max(s, axis=1, keepdims=True))
    alpha = jnp.exp(m_prev - m_new)
    p = jnp.exp(s - m_new[:, :1])
    l_ref[...] = alpha * l_ref[...] + jnp.sum(p, axis=1, keepdims=True)
    acc_ref[...] = alpha * acc_ref[...] + jnp.dot(p.astype(BF16), v, preferred_element_type=F32)
    m_ref[...] = m_new


def _head_finish(o, g, lam_init):
    ms = jnp.mean(o * o, axis=-1, keepdims=True)
    return o * lax.rsqrt(ms + RMS_EPS) * g * (1.0 - lam_init)


def _flash_kernel(qi_ref, ki_ref, lam_ref, q_ref, k_ref, v_ref, g_ref, o_ref,
                  m1, l1, a1, m2, l2, a2, *, lam_init):
    step = pl.program_id(2)
    qi, ki = qi_ref[step], ki_ref[step]

    @pl.when(ki == 0)
    def _():
        for m, l, a in ((m1, l1, a1), (m2, l2, a2)):
            m[...] = jnp.full_like(m, NEG_INF)
            l[...] = jnp.zeros_like(l)
            a[...] = jnp.zeros_like(a)

    def update(masked):
        q, k, v = q_ref[0], k_ref[0], v_ref[0]
        lane = lax.broadcasted_iota(jnp.int32, q.shape, 1)
        zero = jnp.zeros_like(q)
        s1 = _nt_dot(jnp.where(lane < HEAD_DK, q, zero), k)
        s2 = _nt_dot(jnp.where(lane >= HEAD_DK, q, zero), k)
        if masked:
            keep = (lax.broadcasted_iota(jnp.int32, s1.shape, 0)
                    >= lax.broadcasted_iota(jnp.int32, s1.shape, 1))
            s1 = jnp.where(keep, s1, NEG_INF)
            s2 = jnp.where(keep, s2, NEG_INF)
        _softmax_update(s1, v, m1, l1, a1)
        _softmax_update(s2, v, m2, l2, a2)

    @pl.when(ki < qi)
    def _():
        update(False)

    @pl.when(ki == qi)
    def _():
        update(True)
        o = a1[...] / l1[...] - lam_ref[0] * (a2[...] / l2[...])
        o_ref[0] = _head_finish(o, g_ref[...], lam_init).astype(BF16)


def _flash_attention(q, k, v, g_head, lam, lam_init, batch, seq, blk):
    nb = seq // blk
    pairs = [(a, b) for a in range(nb) for b in range(a + 1)]
    qi_tab = jnp.asarray([p[0] for p in pairs], jnp.int32)
    ki_tab = jnp.asarray([p[1] for p in pairs], jnp.int32)
    q3, k3, v3 = (t.reshape(batch, seq, QK_COLS) for t in (q, k, v))
    grid_spec = pltpu.PrefetchScalarGridSpec(
        num_scalar_prefetch=2,
        grid=(batch, N_HEADS, len(pairs)),
        in_specs=[
            pl.BlockSpec(memory_space=pltpu.SMEM),
            pl.BlockSpec((1, blk, LANES), lambda b, h, s, qt, kt: (b, qt[s], h)),
            pl.BlockSpec((1, blk, LANES), lambda b, h, s, qt, kt: (b, kt[s], h)),
            pl.BlockSpec((1, blk, LANES), lambda b, h, s, qt, kt: (b, kt[s], h)),
            pl.BlockSpec((1, LANES), lambda b, h, s, qt, kt: (0, 0)),
        ],
        out_specs=pl.BlockSpec((1, blk, LANES), lambda b, h, s, qt, kt: (b, qt[s], h)),
        scratch_shapes=[pltpu.VMEM((blk, 1), F32), pltpu.VMEM((blk, 1), F32),
                        pltpu.VMEM((blk, HEAD_DV), F32)] * 2,
    )
    out = pl.pallas_call(
        functools.partial(_flash_kernel, lam_init=lam_init),
        grid_spec=grid_spec,
        out_shape=jax.ShapeDtypeStruct((batch, seq, ATTN_WIDTH), BF16),
        compiler_params=_params(("parallel", "parallel", "arbitrary")),
        name="flash_attention",
    )(qi_tab, ki_tab, lam, q3, k3, v3, g_head)
    return out.reshape(batch * seq, ATTN_WIDTH)


def _decode_kernel(pt_ref, lam_ref, q_ref, k_ref, v_ref, kn_ref, vn_ref, g_ref, o_ref,
                   m_ref, l_ref, acc_ref, *, n_pages, n_new, lam_init):
    page = pl.program_id(1)
    rows = q_ref.shape[1]
    half = rows // 2

    @pl.when(page == 0)
    def _():
        m_ref[...] = jnp.full_like(m_ref, NEG_INF)
        l_ref[...] = jnp.zeros_like(l_ref)
        acc_ref[...] = jnp.zeros_like(acc_ref)

    row_head = (lax.broadcasted_iota(jnp.int32, (rows, LANES), 0) % half) // n_new

    def update(s, vflat):
        m_prev = m_ref[...]
        m_new = jnp.maximum(m_prev, jnp.max(s, axis=1, keepdims=True))
        alpha = jnp.exp(m_prev - m_new)
        p = jnp.exp(s - m_new)
        l_ref[...] = alpha * l_ref[...] + jnp.sum(p, axis=1, keepdims=True)
        full = jnp.dot(p.astype(BF16), vflat, preferred_element_type=F32)
        pv = jnp.zeros((rows, HEAD_DV), F32)
        for h in range(N_HEADS):
            pv = pv + jnp.where(row_head == h, full[:, h * HEAD_DV:(h + 1) * HEAD_DV], 0.0)
        acc_ref[...] = alpha * acc_ref[...] + pv
        m_ref[...] = m_new

    @pl.when(page < n_pages)
    def _():
        kflat = jnp.concatenate([k_ref[:, h, :] for h in range(N_HEADS)], axis=1).astype(BF16)
        vflat = jnp.concatenate([v_ref[:, h, :] for h in range(N_HEADS)], axis=1).astype(BF16)
        update(_nt_dot(q_ref[0], kflat), vflat)

    @pl.when(page == n_pages)
    def _():
        s = _nt_dot(q_ref[0], kn_ref[0])
        t_q = lax.broadcasted_iota(jnp.int32, s.shape, 0) % n_new
        t_k = lax.broadcasted_iota(jnp.int32, s.shape, 1)
        s = jnp.where((t_k <= t_q) & (t_k < n_new), s, NEG_INF)
        update(s, vn_ref[0])
        o = acc_ref[...] / l_ref[...]
        o = o[:half] - lam_ref[0] * o[half:]
        o_ref[0] = _head_finish(o, g_ref[...], lam_init).astype(BF16)


def _decode_attention(q, k_new, v_new, cache_k, cache_v, page_table, g_head, lam, lam_init, n_dec, n_new):
    n_pages = page_table.shape[1]
    rows = 2 * N_HEADS * n_new
    q5 = q.reshape(n_dec, n_new, N_HEADS, 2, HEAD_DK)
    eye_h = jnp.eye(N_HEADS, dtype=BF16)
    eye_m = jnp.eye(2, dtype=BF16)
    q_rows = jnp.einsum("bthmd,hH,mM->bmhtHMd", q5, eye_h, eye_m).reshape(n_dec, rows, QK_COLS)
    pad = lambda t: jnp.pad(t.reshape(n_dec, n_new, QK_COLS), ((0, 0), (0, NEW_ROWS - n_new), (0, 0)))
    kn, vn = pad(k_new), pad(v_new)
    last = n_pages - 1
    grid_spec = pltpu.PrefetchScalarGridSpec(
        num_scalar_prefetch=1,
        grid=(n_dec, n_pages + 1),
        in_specs=[
            pl.BlockSpec(memory_space=pltpu.SMEM),
            pl.BlockSpec((1, rows, QK_COLS), lambda b, p, pt: (b, 0, 0)),
            pl.BlockSpec((None, PAGE_SIZE, N_HEADS, LANES),
                         lambda b, p, pt: (pt[b * n_pages + jnp.minimum(p, last)], 0, 0, 0)),
            pl.BlockSpec((None, PAGE_SIZE, N_HEADS, LANES),
                         lambda b, p, pt: (pt[b * n_pages + jnp.minimum(p, last)], 0, 0, 0)),
            pl.BlockSpec((1, NEW_ROWS, QK_COLS), lambda b, p, pt: (b, 0, 0)),
            pl.BlockSpec((1, NEW_ROWS, QK_COLS), lambda b, p, pt: (b, 0, 0)),
            pl.BlockSpec((1, LANES), lambda b, p, pt: (0, 0)),
        ],
        out_specs=pl.BlockSpec((1, rows // 2, HEAD_DV), lambda b, p, pt: (b, 0, 0)),
        scratch_shapes=[pltpu.VMEM((rows, 1), F32), pltpu.VMEM((rows, 1), F32),
                        pltpu.VMEM((rows, HEAD_DV), F32)],
    )
    out = pl.pallas_call(
        functools.partial(_decode_kernel, n_pages=n_pages, n_new=n_new, lam_init=lam_init),
        grid_spec=grid_spec,
        out_shape=jax.ShapeDtypeStruct((n_dec, rows // 2, HEAD_DV), BF16),
        compiler_params=_params(("parallel", "arbitrary")),
        name="decode_attention",
    )(page_table.reshape(-1), lam, q_rows, cache_k, cache_v, kn, vn, g_head)
    out = out.reshape(n_dec, N_HEADS, n_new, HEAD_DV).transpose(0, 2, 1, 3)
    return out.reshape(n_dec * n_new, ATTN_WIDTH)


def _ssm_tables(a_re, a_im, log_dt, b_re, b_im, c_re, c_im, d_skip, n_pow):
    dt = jnp.exp(log_dt)[:, None]
    mag = jnp.exp(a_re * dt)
    ab_re, ab_im = mag * jnp.cos(a_im * dt), mag * jnp.sin(a_im * dt)
    den = a_re * a_re + a_im * a_im
    f_re = ((ab_re - 1.0) * a_re + ab_im * a_im) / den
    f_im = (ab_im * a_re - (ab_re - 1.0) * a_im) / den
    bb_re = f_re[..., None] * b_re - f_im[..., None] * b_im
    bb_im = f_re[..., None] * b_im + f_im[..., None] * b_re
    eye = jnp.eye(SSM_GB, dtype=F32)

    def b_blocks(bb):
        bb = bb.reshape(SSM_NB, SSM_GB, SSM_STATE, SSM_GROUP)
        return jnp.einsum("agpc,gh->agchp", bb, eye).reshape(SSM_NB, LANES, SSM_SW)

    def c_blocks(cc):
        cc = cc.reshape(SSM_NB, SSM_GB, SSM_GROUP, SSM_STATE)
        return jnp.einsum("agcp,gh->agphc", cc, eye).reshape(SSM_NB, SSM_SW, LANES)

    bbd = jnp.concatenate([b_blocks(bb_re), b_blocks(bb_im)], axis=2).astype(BF16)
    cbd = jnp.concatenate([c_blocks(c_re), c_blocks(-c_im)], axis=1).astype(BF16)
    d = d_skip.reshape(SSM_NB, 1, LANES)
    pr, pi = ab_re[None], ab_im[None]
    while pr.shape[0] < n_pow:
        tr, ti = pr[-1:], pi[-1:]
        pr, pi = (jnp.concatenate([pr, pr * tr - pi * ti], axis=0),
                  jnp.concatenate([pi, pr * ti + pi * tr], axis=0))
    lay = lambda t: t[:n_pow].reshape(n_pow, SSM_NB, SSM_SW).transpose(1, 0, 2)
    apow = jnp.concatenate([lay(pr), lay(pi)], axis=2)
    return bbd, cbd, d, apow


def _shift_rows(x, d):
    n = x.shape[0]
    if d % SUBLANES == 0:
        return jnp.concatenate([jnp.zeros((d, x.shape[1]), x.dtype), x[:n - d]], axis=0)
    rolled = pltpu.roll(x, d, axis=0)
    row = lax.broadcasted_iota(jnp.int32, x.shape, 0)
    return jnp.where(row < d, 0.0, rolled)


def _ssm_prompt_kernel(u_ref, b_ref, c_ref, d_ref, ap_ref, y_ref, sre_ref, sim_ref, h_ref):
    chunk = pl.program_id(2)

    @pl.when(chunk == 0)
    def _():
        h_ref[...] = jnp.zeros_like(h_ref)

    u = u_ref[0]
    length = u.shape[0]
    bu = jnp.dot(u.astype(BF16), b_ref[0], preferred_element_type=F32)
    re, im = bu[:, :SSM_SW], bu[:, SSM_SW:]
    ap = ap_ref[0]
    d = 1
    while d < length:
        ar, ai = ap[d - 1:d, :SSM_SW], ap[d - 1:d, SSM_SW:]
        sr, si = _shift_rows(re, d), _shift_rows(im, d)
        re, im = re + ar * sr - ai * si, im + ar * si + ai * sr
        d *= 2
    hr, hi = h_ref[0:1, :SSM_SW], h_ref[0:1, SSM_SW:]
    pr, pi = ap[:, :SSM_SW], ap[:, SSM_SW:]
    re, im = re + pr * hr - pi * hi, im + pr * hi + pi * hr
    h_ref[0:1, :SSM_SW] = re[length - 1:length]
    h_ref[0:1, SSM_SW:] = im[length - 1:length]
    hcat = jnp.concatenate([re, im], axis=1).astype(BF16)
    y = jnp.dot(hcat, c_ref[0], preferred_element_type=F32) + d_ref[0] * u
    y_ref[0] = jax.nn.gelu(y).astype(BF16)

    @pl.when(chunk == pl.num_programs(2) - 1)
    def _():
        sre_ref[0, 0] = re[length - 1:length]
        sim_ref[0, 0] = im[length - 1:length]


def _ssm_prompt(z, tables, batch, seq):
    bbd, cbd, d, apow = tables
    length = apow.shape[1]
    z3 = z.reshape(batch, seq, IN_COLS)
    u_blk = V_END // LANES
    y, sre, sim = pl.pallas_call(
        _ssm_prompt_kernel,
        grid=(batch, SSM_NB, seq // length),
        in_specs=[
            pl.BlockSpec((1, length, LANES), lambda b, g, c: (b, c, u_blk + g)),
            pl.BlockSpec((1, LANES, 2 * SSM_SW), lambda b, g, c: (g, 0, 0)),
            pl.BlockSpec((1, 2 * SSM_SW, LANES), lambda b, g, c: (g, 0, 0)),
            pl.BlockSpec((1, 1, LANES), lambda b, g, c: (g, 0, 0)),
            pl.BlockSpec((1, length, 2 * SSM_SW), lambda b, g, c: (g, 0, 0)),
        ],
        out_specs=[
            pl.BlockSpec((1, length, LANES), lambda b, g, c: (b, c, g)),
            pl.BlockSpec((1, 1, 1, SSM_SW), lambda b, g, c: (b, g, 0, 0)),
            pl.BlockSpec((1, 1, 1, SSM_SW), lambda b, g, c: (b, g, 0, 0)),
        ],
        out_shape=[jax.ShapeDtypeStruct((batch, seq, SSM_WIDTH), BF16),
                   jax.ShapeDtypeStruct((batch, SSM_NB, 1, SSM_SW), F32),
                   jax.ShapeDtypeStruct((batch, SSM_NB, 1, SSM_SW), F32)],
        scratch_shapes=[pltpu.VMEM((SUBLANES, 2 * SSM_SW), F32)],
        compiler_params=_params(("parallel", "parallel", "arbitrary")),
        name="ssm_prompt",
    )(z3, bbd, cbd, d, apow)
    state = lambda s: s.reshape(batch, SSM_GROUPS, SSM_STATE)
    return y.reshape(batch * seq, SSM_WIDTH), state(sre), state(sim)


def _ssm_step_kernel(u_ref, b_ref, c_ref, d_ref, ap_ref, hre_ref, him_ref, y_ref, sre_ref, sim_ref):
    ar, ai = ap_ref[0, 0:1, :SSM_SW], ap_ref[0, 0:1, SSM_SW:]
    re, im = hre_ref[...], him_ref[...]
    bmat = b_ref[0]
    for t in range(u_ref.shape[0]):
        u = u_ref[t]
        u_hi = u.astype(BF16)
        u_lo = (u - u_hi.astype(F32)).astype(BF16)
        bu = (jnp.dot(u_hi, bmat, preferred_element_type=F32)
              + jnp.dot(u_lo, bmat, preferred_element_type=F32))
        re, im = (ar * re - ai * im + bu[:, :SSM_SW], ar * im + ai * re + bu[:, SSM_SW:])
        hcat = jnp.concatenate([re, im], axis=1).astype(BF16)
        y = jnp.dot(hcat, c_ref[0], preferred_element_type=F32) + d_ref[0] * u
        y_ref[t] = jax.nn.gelu(y).astype(BF16)
    sre_ref[...] = re
    sim_ref[...] = im


def _ssm_step(z, tables, h_re, h_im, batch, steps):
    bbd, cbd, d, apow = tables
    u = z[:, V_END:U_END].reshape(batch, steps, SSM_WIDTH).transpose(1, 0, 2)
    flat = lambda s: s.reshape(batch, SSM_GROUPS * SSM_STATE)
    y, sre, sim = pl.pallas_call(
        _ssm_step_kernel,
        grid=(SSM_NB,),
        in_specs=[
            pl.BlockSpec((steps, batch, LANES), lambda g: (0, 0, g)),
            pl.BlockSpec((1, LANES, 2 * SSM_SW), lambda g: (g, 0, 0)),
            pl.BlockSpec((1, 2 * SSM_SW, LANES), lambda g: (g, 0, 0)),
            pl.BlockSpec((1, 1, LANES), lambda g: (g, 0, 0)),
            pl.BlockSpec((1, SUBLANES, 2 * SSM_SW), lambda g: (g, 0, 0)),
            pl.BlockSpec((batch, SSM_SW), lambda g: (0, g)),
            pl.BlockSpec((batch, SSM_SW), lambda g: (0, g)),
        ],
        out_specs=[
            pl.BlockSpec((steps, batch, LANES), lambda g: (0, 0, g)),
            pl.BlockSpec((batch, SSM_SW), lambda g: (0, g)),
            pl.BlockSpec((batch, SSM_SW), lambda g: (0, g)),
        ],
        out_shape=[jax.ShapeDtypeStruct((steps, batch, SSM_WIDTH), BF16),
                   jax.ShapeDtypeStruct((batch, SSM_GROUPS * SSM_STATE), F32),
                   jax.ShapeDtypeStruct((batch, SSM_GROUPS * SSM_STATE), F32)],
        compiler_params=_params(("parallel",)),
        name="ssm_step",
    )(u, bbd, cbd, d, apow, flat(h_re), flat(h_im))
    y = y.transpose(1, 0, 2).reshape(batch * steps, SSM_WIDTH)
    state = lambda s: s.reshape(batch, SSM_GROUPS, SSM_STATE)
    return y, state(sre), state(sim)


def _merge_kernel(att_ref, ys_ref, za_ref, zs_ref, wup_ref, wga_ref, wgb_ref, o_ref):
    ys = ys_ref[...]
    branch_a = jnp.dot(att_ref[...], wup_ref[...], preferred_element_type=F32)
    glu_a = jnp.dot(ys, wga_ref[...], preferred_element_type=F32)
    glu_b = jnp.dot(ys, wgb_ref[...], preferred_element_type=F32)
    branch_s = glu_a * jax.nn.sigmoid(glu_b)
    merged = jax.nn.sigmoid(za_ref[...]) * branch_a + jax.nn.sigmoid(zs_ref[...]) * branch_s
    o_ref[...] = merged.astype(BF16)


def _merge(att, ys, z, w_up, w_glu, tm, tn):
    n = att.shape[0]
    nj = D_MODEL // tn
    act = pl.BlockSpec((tm, ATTN_WIDTH), lambda j, i: (i, 0))
    gate = lambda off: pl.BlockSpec((tm, tn), lambda j, i: (i, off + j))
    wcol = lambda off: pl.BlockSpec((ATTN_WIDTH, tn), lambda j, i: (0, off + j))
    return pl.pallas_call(
        _merge_kernel,
        grid=(nj, n // tm),
        in_specs=[act, act, gate(U_END // tn), gate((U_END + D_MODEL) // tn),
                  wcol(0), wcol(0), wcol(nj)],
        out_specs=pl.BlockSpec((tm, tn), lambda j, i: (i, j)),
        out_shape=jax.ShapeDtypeStruct((n, D_MODEL), BF16),
        compiler_params=_params(("parallel", "parallel")),
        name="merge",
    )(att, ys, z, z, w_up, w_glu, w_glu)


def _out_proj_kernel(x_ref, m_ref, w_ref, g_ref, x1_ref, hn_ref):
    x1 = x_ref[...] + jnp.dot(m_ref[...], w_ref[...], preferred_element_type=F32)
    x1_ref[...] = x1
    ms = jnp.mean(x1 * x1, axis=-1, keepdims=True)
    hn_ref[...] = (x1 * lax.rsqrt(ms + RMS_EPS) * g_ref[...]).astype(BF16)


def _out_proj(x, merged, w_out, g_ffn, tm):
    n = x.shape[0]
    row = pl.BlockSpec((tm, D_MODEL), lambda i: (i, 0))
    return pl.pallas_call(
        _out_proj_kernel,
        grid=(n // tm,),
        in_specs=[row, row, pl.BlockSpec((D_MODEL, D_MODEL), lambda i: (0, 0)),
                  pl.BlockSpec((1, D_MODEL), lambda i: (0, 0))],
        out_specs=[row, row],
        out_shape=[jax.ShapeDtypeStruct((n, D_MODEL), F32),
                   jax.ShapeDtypeStruct((n, D_MODEL), BF16)],
        compiler_params=_params(("parallel",)),
        name="out_proj",
    )(x, merged, w_out, g_ffn)


def _top_rows(s, k):
    n = s.shape[0]
    row = lax.broadcasted_iota(jnp.int32, s.shape, 0)
    vals, idxs = [], []
    for _ in range(k):
        best = jnp.max(s, axis=0, keepdims=True)
        pick = jnp.min(jnp.where(s == best, row, n), axis=0, keepdims=True)
        vals.append(best)
        idxs.append(pick)
        s = jnp.where(row == pick, -jnp.inf, s)
    return jnp.concatenate(vals, axis=0), jnp.concatenate(idxs, axis=0)


def _router_kernel(h_ref, wq_ref, sk_ref, a_ref, b_ref, g_ref):
    tm = h_ref.shape[0]
    k = PEER_TOPK
    q = jnp.dot(h_ref[...], wq_ref[...], preferred_element_type=F32).astype(BF16)
    half = PEER_DKEY // 2

    n_rows = k + 8 * SUBLANES
    r = lax.broadcasted_iota(jnp.int32, (n_rows, tm), 0)
    p_mid = ((r - k) >> 3) + 1
    q_mid = (r - k) & 7
    valid = (r < k) | (r >= k + 7 * SUBLANES) | ((p_mid + 1) * (q_mid + 1) <= k)

    def cand_rows(first, second):
        parts = [jnp.broadcast_to(first[0:1], (k, tm)) if second is None else first[0:1] + second[0:k]]
        for p in range(1, 8):
            parts.append(jnp.broadcast_to(first[p:p + 1], (SUBLANES, tm)) if second is None
                         else first[p:p + 1] + second[0:SUBLANES])
        parts.append(first[8:16] if second is None else first[8:16] + second[0:1])
        return jnp.concatenate(parts, axis=0)

    def second_rows(second):
        parts = [second[0:k]] + [second[0:SUBLANES]] * 7
        parts.append(jnp.broadcast_to(second[0:1], (SUBLANES, tm)))
        return jnp.concatenate(parts, axis=0)

    for h in range(PEER_HEADS):
        tops = []
        for m in range(2):
            c = 2 * h + m
            s = _nt_dot(sk_ref[m], q[:, c * half:(c + 1) * half])
            tops.append(_top_rows(s, k))
        (v1, i1), (v2, i2) = tops
        cand = jnp.where(valid, cand_rows(v1, v2), -jnp.inf)
        first_idx = cand_rows(i1, None)
        second_idx = second_rows(i2)
        fv, frow = _top_rows(cand, k)
        sel_a, sel_b = [], []
        for j in range(k):
            hit = r == frow[j:j + 1]
            sel_a.append(jnp.max(jnp.where(hit, first_idx, -1), axis=0, keepdims=True))
            sel_b.append(jnp.max(jnp.where(hit, second_idx, -1), axis=0, keepdims=True))
        e = jnp.exp(fv - fv[0:1])
        gate = e / jnp.sum(e, axis=0, keepdims=True)
        a_ref[h * k:(h + 1) * k, :] = jnp.concatenate(sel_a, axis=0)
        b_ref[h * k:(h + 1) * k, :] = jnp.concatenate(sel_b, axis=0)
        g_ref[h * k:(h + 1) * k, :] = gate


def _router(hn, w_query, sub_keys, tm):
    n = hn.shape[0]
    slots = PEER_HEADS * PEER_TOPK
    out = pl.BlockSpec((slots, tm), lambda i: (0, i))
    return pl.pallas_call(
        _router_kernel,
        grid=(n // tm,),
        in_specs=[pl.BlockSpec((tm, D_MODEL), lambda i: (i, 0)),
                  pl.BlockSpec((D_MODEL, PEER_HEADS * PEER_DKEY), lambda i: (0, 0)),
                  pl.BlockSpec((2, PEER_KEYS, PEER_DKEY // 2), lambda i: (0, 0, 0))],
        out_specs=[out, out, out],
        out_shape=[jax.ShapeDtypeStruct((slots, n), jnp.int32),
                   jax.ShapeDtypeStruct((slots, n), jnp.int32),
                   jax.ShapeDtypeStruct((slots, n), F32)],
        compiler_params=_params(("parallel",)),
        name="router",
    )(hn, w_query, sub_keys)


def _wbuild_kernel(a_ref, b_ref, g_ref, w_ref):
    tm = a_ref.shape[0]
    idx = lax.broadcasted_iota(jnp.int32, (PEER_KEYS, a_ref.shape[1]), 0)

    def body(n, carry):
        a_row = a_ref[pl.ds(n, 1), :]
        b_row = b_ref[pl.ds(n, 1), :]
        g_row = g_ref[pl.ds(n, 1), :]
        first = jnp.where(a_row == idx, g_row, 0.0).astype(BF16)
        second = jnp.where(b_row == idx, 1.0, 0.0).astype(BF16)
        w_ref[n] = _nt_dot(first, second)
        return carry

    lax.fori_loop(0, tm, body, 0)


def _wbuild(a, b, g, tm):
    n, slots = a.shape
    row = pl.BlockSpec((tm, slots), lambda i: (i, 0))
    return pl.pallas_call(
        _wbuild_kernel,
        grid=(n // tm,),
        in_specs=[row, row, row],
        out_specs=pl.BlockSpec((tm, PEER_KEYS, PEER_KEYS), lambda i: (i, 0, 0)),
        out_shape=jax.ShapeDtypeStruct((n, PEER_KEYS, PEER_KEYS), F32),
        compiler_params=_params(("parallel",)),
        name="wbuild",
    )(a, b, g)


def _experts_kernel(h_ref, ut_ref, v_ref, w_ref, o_ref):
    @pl.when(pl.program_id(1) == 0)
    def _():
        o_ref[...] = jnp.zeros_like(o_ref)

    s = jnp.dot(h_ref[...], ut_ref[...], preferred_element_type=F32)
    w = jnp.concatenate([w_ref[:, r, :] for r in range(w_ref.shape[1])], axis=1)
    c = (jax.nn.gelu(s) * w).astype(BF16)
    o_ref[...] += jnp.dot(c, v_ref[...], preferred_element_type=F32)


def _experts(hn, u_t, v, w, tm):
    n = hn.shape[0]
    te = SUBLANES * PEER_KEYS
    return pl.pallas_call(
        _experts_kernel,
        grid=(n // tm, PEER_EXPERTS // te),
        in_specs=[pl.BlockSpec((tm, D_MODEL), lambda i, j: (i, 0)),
                  pl.BlockSpec((D_MODEL, te), lambda i, j: (0, j)),
                  pl.BlockSpec((te, D_MODEL), lambda i, j: (j, 0)),
                  pl.BlockSpec((tm, SUBLANES, PEER_KEYS), lambda i, j: (i, j, 0))],
        out_specs=pl.BlockSpec((tm, D_MODEL), lambda i, j: (i, 0)),
        out_shape=jax.ShapeDtypeStruct((n, D_MODEL), F32),
        compiler_params=_params(("parallel", "arbitrary")),
        name="experts",
    )(hn, u_t, v, w)


def _ple_kernel(x1_ref, peer_ref, p_ref, wp_ref, wg_ref, g_ref, o_ref):
    x2 = x1_ref[...] + peer_ref[...]
    ms = jnp.mean(x2 * x2, axis=-1, keepdims=True)
    hn = (x2 * lax.rsqrt(ms + RMS_EPS) * g_ref[...]).astype(BF16)
    emb = jnp.dot(p_ref[...].astype(BF16), wp_ref[...], preferred_element_type=F32)
    gate = jax.nn.sigmoid(jnp.dot(hn, wg_ref[...], preferred_element_type=F32))
    o_ref[...] = x2 + emb * gate


def _ple(x1, peer, p, w_ple, w_gate, g_ple, tm):
    n = x1.shape[0]
    row = pl.BlockSpec((tm, D_MODEL), lambda i: (i, 0))
    return pl.pallas_call(
        _ple_kernel,
        grid=(n // tm,),
        in_specs=[row, row, pl.BlockSpec((tm, PLE_DIM), lambda i: (i, 0)),
                  pl.BlockSpec((PLE_DIM, D_MODEL), lambda i: (0, 0)),
                  pl.BlockSpec((D_MODEL, D_MODEL), lambda i: (0, 0)),
                  pl.BlockSpec((1, D_MODEL), lambda i: (0, 0))],
        out_specs=row,
        out_shape=jax.ShapeDtypeStruct((n, D_MODEL), F32),
        compiler_params=_params(("parallel",)),
        name="ple",
    )(x1, peer, p, w_ple, w_gate, g_ple)


def _tile(n, pref):
    return pref if n % pref == 0 else n


def _layer(x, p, pos_rows, attention, ssm, wts):
    n = x.shape[0]
    z = _in_proj(x, wts["g_mix"], wts["w_in"], _tile(n, 512), 1024)
    cos, sin_lo, sin_hi = _rope_tables(pos_rows)
    q, k_f32, k_bf, v_bf = _qkv_prep(z, wts["g_q"], wts["g_k"], wts["bd"], cos, sin_lo, sin_hi,
                                     _tile(cos.shape[0], 256))
    att = attention(q, k_bf, v_bf)
    ys, s_re, s_im = ssm(z)
    merged = _merge(att, ys, z, wts["w_attn_up"], wts["w_glu"], _tile(n, 512), 512)
    x1, hn = _out_proj(x, merged, wts["w_out"], wts["g_ffn"], _tile(n, 256))
    a_t, b_t, g_t = _router(hn, wts["peer_w_query"], wts["peer_sub_keys"], 128)
    w = _wbuild(a_t.T, b_t.T, g_t.T, _tile(n, 64))
    peer = _experts(hn, wts["peer_u_t"], wts["peer_v"], w, _tile(n, 512))
    y = _ple(x1, peer, p, wts["w_ple"], wts["w_ple_gate"], wts["g_ple"], _tile(n, 256))
    k_new = k_f32.reshape(n, N_HEADS, 2 * HEAD_DK)
    v_new = z[:, 2 * QK_COLS:V_END].reshape(n, N_HEADS, HEAD_DV)
    return y, k_new, v_new, s_re, s_im


def kernel(x_prompt, x_sample, cache_k, cache_v, state_ssm_re, state_ssm_im, page_table, p_prompt, p_sample, g_mix, w_in, g_q, g_k, lambda_q, lambda_k, g_head, w_attn_up, ssm_a_re, ssm_a_im, ssm_log_dt, ssm_b_re, ssm_b_im, ssm_c_re, ssm_c_im, ssm_d, w_glu, w_out, g_ffn, peer_w_query, peer_sub_keys, peer_u, peer_v, g_ple, w_ple, w_ple_gate):
    depth = w_in.shape[0]
    assert depth == 1
    batch, seq, _ = x_prompt.shape
    n_dec, n_new, _ = x_sample.shape
    past_len = page_table.shape[1] * PAGE_SIZE
    i = 0
    lam_init = 0.8 - 0.6 * math.exp(-0.3 * i)

    row = lambda t: t.reshape(1, -1)
    bd = jnp.kron(jnp.eye(2, dtype=F32), jnp.ones((HEAD_DK, HEAD_DK), F32)).astype(BF16)
    wts = {
        "g_mix": row(g_mix[i]), "w_in": w_in[i].astype(BF16),
        "g_q": row(g_q[i]), "g_k": row(g_k[i]), "bd": bd,
        "w_attn_up": w_attn_up[i].astype(BF16), "w_glu": w_glu[i].astype(BF16),
        "w_out": w_out[i].astype(BF16), "g_ffn": row(g_ffn[i]),
        "peer_w_query": peer_w_query[i].astype(BF16), "peer_sub_keys": peer_sub_keys[i].astype(BF16),
        "peer_u_t": peer_u[i].T.astype(BF16), "peer_v": peer_v[i].astype(BF16),
        "g_ple": row(g_ple[i]), "w_ple": w_ple[i].astype(BF16), "w_ple_gate": w_ple_gate[i].astype(BF16),
    }
    g_head2 = row(g_head[i])
    lq, lk = lambda_q[i].astype(F32), lambda_k[i].astype(F32)
    lam = (jnp.exp(jnp.sum(lq[0] * lk[0])) - jnp.exp(jnp.sum(lq[1] * lk[1])) + lam_init).reshape(1)
    ssm_args = (ssm_a_re[i], ssm_a_im[i], ssm_log_dt[i], ssm_b_re[i], ssm_b_im[i],
                ssm_c_re[i], ssm_c_im[i], ssm_d[i])
    tables_prompt = _ssm_tables(*ssm_args, SSM_CHUNK)
    tables_step = _ssm_tables(*ssm_args, SUBLANES)

    pos_prompt = jnp.arange(seq, dtype=jnp.int32)
    y_p, k_p, v_p, sre_p, sim_p = _layer(
        x_prompt.reshape(batch * seq, D_MODEL), p_prompt[i].reshape(batch * seq, PLE_DIM), pos_prompt,
        lambda q, k, v: _flash_attention(q, k, v, g_head2, lam, lam_init, batch, seq, 512),
        lambda z: _ssm_prompt(z, tables_prompt, batch, seq),
        wts)

    pos_sample = jnp.tile(past_len + jnp.arange(n_new, dtype=jnp.int32), n_dec)
    y_s, k_s, v_s, sre_s, sim_s = _layer(
        x_sample.reshape(n_dec * n_new, D_MODEL), p_sample[i].reshape(n_dec * n_new, PLE_DIM), pos_sample,
        lambda q, k, v: _decode_attention(q, k, v, cache_k[i], cache_v[i], page_table, g_head2, lam,
                                          lam_init, n_dec, n_new),
        lambda z: _ssm_step(z, tables_step, state_ssm_re[i], state_ssm_im[i], n_dec, n_new),
        wts)

    lead = lambda t, *shape: t.reshape(1, *shape)
    return (y_p.reshape(batch, seq, D_MODEL), y_s.reshape(n_dec, n_new, D_MODEL),
            lead(k_p, batch, seq, N_HEADS, 2 * HEAD_DK), lead(v_p, batch, seq, N_HEADS, HEAD_DV),
            lead(sre_p, batch, SSM_GROUPS, SSM_STATE), lead(sim_p, batch, SSM_GROUPS, SSM_STATE),
            lead(k_s, n_dec, n_new, N_HEADS, 2 * HEAD_DK), lead(v_s, n_dec, n_new, N_HEADS, HEAD_DV),
            lead(sre_s, n_dec, SSM_GROUPS, SSM_STATE), lead(sim_s, n_dec, SSM_GROUPS, SSM_STATE))
```

```python
import functools
import math

import jax
import jax.numpy as jnp
from jax import lax
from jax.experimental import pallas as pl
from jax.experimental.pallas import tpu as pltpu

F32 = jnp.float32
BF16 = jnp.bfloat16

D_MODEL = 2048
PAGE_SIZE = 128
N_HEADS = 8
HEAD_DK = 64
HEAD_DV = 2 * HEAD_DK
QK_COLS = N_HEADS * 2 * HEAD_DK
ATTN_WIDTH = N_HEADS * HEAD_DV
ROPE_DIMS = HEAD_DK // 4
ROPE_THETA = 500000.0
NEG_INF = -1e30
SSM_WIDTH = D_MODEL // 2
SSM_GROUP = 16
SSM_GROUPS = SSM_WIDTH // SSM_GROUP
SSM_STATE = 64
PEER_HEADS = 8
PEER_KEYS = 128
PEER_EXPERTS = PEER_KEYS * PEER_KEYS
PEER_DKEY = 256
PEER_TOPK = 16
PLE_DIM = 256
RMS_EPS = 1e-6
V_END = 2 * QK_COLS + ATTN_WIDTH
U_END = V_END + SSM_WIDTH
IN_COLS = U_END + 2 * D_MODEL

LANES = 128
SUBLANES = 8
VMEM_LIMIT = 52 * 1024 * 1024
SSM_GB = LANES // SSM_GROUP
SSM_NB = SSM_GROUPS // SSM_GB
SSM_SW = SSM_GB * SSM_STATE
SSM_CHUNK = 128
DECODE_PAGES = 8
FLASH_BLOCK = 512
WBUILD_UNROLL = 8
EXPERT_TILE = 1024


def _params(sem):
    return pltpu.CompilerParams(dimension_semantics=sem, vmem_limit_bytes=VMEM_LIMIT)


def _nt_dot(a, b):
    return lax.dot_general(a, b, (((1,), (1,)), ((), ())), preferred_element_type=F32)


def _in_proj_kernel(x_ref, g_ref, w_ref, o_ref, h_ref):
    @pl.when(pl.program_id(1) == 0)
    def _():
        x = x_ref[...]
        ms = jnp.mean(x * x, axis=-1, keepdims=True)
        h_ref[...] = (x * lax.rsqrt(ms + RMS_EPS) * g_ref[...]).astype(BF16)

    o_ref[...] = jnp.dot(h_ref[...], w_ref[...], preferred_element_type=F32)


def _in_proj(x, g, w_bf, tm, tn):
    n, d = x.shape
    cols = w_bf.shape[1]
    return pl.pallas_call(
        _in_proj_kernel,
        grid=(n // tm, cols // tn),
        in_specs=[
            pl.BlockSpec((tm, d), lambda i, j: (i, 0)),
            pl.BlockSpec((1, d), lambda i, j: (0, 0)),
            pl.BlockSpec((d, tn), lambda i, j: (0, j)),
        ],
        out_specs=pl.BlockSpec((tm, tn), lambda i, j: (i, j)),
        out_shape=jax.ShapeDtypeStruct((n, cols), F32),
        scratch_shapes=[pltpu.VMEM((tm, d), BF16)],
        compiler_params=_params(("parallel", "arbitrary")),
        name="in_proj",
    )(x, g, w_bf)


def _qkv_prep_kernel(zq_ref, zk_ref, zv_ref, gq_ref, gk_ref, bd_ref, c_ref, s1_ref, s2_ref,
                     q_ref, kf_ref, kb_ref, vb_ref):
    bd = bd_ref[...]
    cos, sin_lo, sin_hi = c_ref[...], s1_ref[...], s2_ref[...]

    def norm_rope(x, g):
        x2 = x * x
        hi = x2.astype(BF16)
        lo = (x2 - hi.astype(F32)).astype(BF16)
        ss = (jnp.dot(hi, bd, preferred_element_type=F32)
              + jnp.dot(lo, bd, preferred_element_type=F32))
        xn = x * lax.rsqrt(ss * (1.0 / HEAD_DK) + RMS_EPS) * g
        fwd = pltpu.roll(xn, LANES - ROPE_DIMS // 2, axis=1)
        bwd = pltpu.roll(xn, ROPE_DIMS // 2, axis=1)
        return xn * cos + fwd * sin_lo + bwd * sin_hi

    for c in range(QK_COLS // LANES):
        sl = slice(c * LANES, (c + 1) * LANES)
        q = norm_rope(zq_ref[:, sl], gq_ref[...])
        q_ref[:, sl] = (q * (HEAD_DK ** -0.5)).astype(BF16)
        k = norm_rope(zk_ref[:, sl], gk_ref[...])
        kf_ref[:, sl] = k
        kb_ref[:, sl] = k.astype(BF16)
    vb_ref[...] = zv_ref[...].astype(BF16)


def _qkv_prep(z, gq, gk, bd, cos, sin_lo, sin_hi, tm):
    n = z.shape[0]
    nt = cos.shape[0] // tm
    row = lambda c: pl.BlockSpec((tm, QK_COLS), lambda i: (i, c))
    vec = pl.BlockSpec((1, LANES), lambda i: (0, 0))
    tab = pl.BlockSpec((tm, LANES), lambda i: (i % nt, 0))
    out = pl.BlockSpec((tm, QK_COLS), lambda i: (i, 0))
    return pl.pallas_call(
        _qkv_prep_kernel,
        grid=(n // tm,),
        in_specs=[row(0), row(1), row(2), vec, vec,
                  pl.BlockSpec((LANES, LANES), lambda i: (0, 0)), tab, tab, tab],
        out_specs=[out, out, out, out],
        out_shape=[jax.ShapeDtypeStruct((n, QK_COLS), BF16),
                   jax.ShapeDtypeStruct((n, QK_COLS), F32),
                   jax.ShapeDtypeStruct((n, QK_COLS), BF16),
                   jax.ShapeDtypeStruct((n, ATTN_WIDTH), BF16)],
        compiler_params=_params(("parallel",)),
        name="qkv_prep",
    )(z, z, z, gq, gk, bd, cos, sin_lo, sin_hi)


def _rope_tables(pos):
    half = ROPE_DIMS // 2
    inv_freq = ROPE_THETA ** (-jnp.arange(half, dtype=F32) / half)
    ang = pos.astype(F32)[:, None] * inv_freq[None, :]
    cos, sin = jnp.cos(ang), jnp.sin(ang)
    ones = jnp.ones((pos.shape[0], HEAD_DK - ROPE_DIMS), F32)
    zeros8 = jnp.zeros_like(sin)
    zeros = jnp.zeros_like(ones)
    c = jnp.concatenate([cos, cos, ones], axis=1)
    s_lo = jnp.concatenate([-sin, zeros8, zeros], axis=1)
    s_hi = jnp.concatenate([zeros8, sin, zeros], axis=1)
    two = lambda a: jnp.concatenate([a, a], axis=1)
    return two(c), two(s_lo), two(s_hi)


def _head_finish(o, g, lam_init):
    ms = jnp.mean(o * o, axis=-1, keepdims=True)
    return o * lax.rsqrt(ms + RMS_EPS) * g * (1.0 - lam_init)


def _flash_kernel(lam_ref, q_ref, k_ref, v_ref, g_ref, o_ref, *, blk, lam_init):
    qi = pl.program_id(2)
    q = q_ref[0]
    lane = lax.broadcasted_iota(jnp.int32, q.shape, 1)
    zero = jnp.zeros_like(q)
    qq = jnp.concatenate([jnp.where(lane < HEAD_DK, q, zero),
                          jnp.where(lane >= HEAD_DK, q, zero)], axis=0)

    def step(ki, carry, masked):
        m, l, acc = carry
        start = pl.multiple_of(ki * blk, blk)
        k = k_ref[0, pl.ds(start, blk), :]
        v = v_ref[0, pl.ds(start, blk), :]
        s = _nt_dot(qq, k)
        if masked:
            r = lax.broadcasted_iota(jnp.int32, s.shape, 0)
            c = lax.broadcasted_iota(jnp.int32, s.shape, 1)
            s = jnp.where(jnp.where(r >= blk, r - blk, r) >= c, s, NEG_INF)
        m_new = jnp.maximum(m, jnp.max(s, axis=1, keepdims=True))
        alpha = jnp.exp(m - m_new)
        p = jnp.exp(s - m_new)
        l = alpha * l + jnp.sum(p, axis=1, keepdims=True)
        acc = alpha * acc + jnp.dot(p.astype(BF16), v, preferred_element_type=F32)
        return m_new, l, acc

    init = (jnp.full((2 * blk, 1), NEG_INF, F32), jnp.zeros((2 * blk, 1), F32),
            jnp.zeros((2 * blk, HEAD_DV), F32))
    carry = lax.fori_loop(0, qi, lambda ki, c: step(ki, c, False), init)
    _, l, acc = step(qi, carry, True)
    o = acc / l
    o = o[:blk] - lam_ref[0] * o[blk:]
    o_ref[0] = _head_finish(o, g_ref[...], lam_init).astype(BF16)


def _flash_attention(q, k, v, g_head, lam, lam_init, batch, seq, blk):
    q3, k3, v3 = (t.reshape(batch, seq, QK_COLS) for t in (q, k, v))
    whole = pl.BlockSpec((1, seq, LANES), lambda b, h, i: (b, 0, h))
    out = pl.pallas_call(
        functools.partial(_flash_kernel, blk=blk, lam_init=lam_init),
        grid=(batch, N_HEADS, seq // blk),
        in_specs=[
            pl.BlockSpec(memory_space=pltpu.SMEM),
            pl.BlockSpec((1, blk, LANES), lambda b, h, i: (b, i, h)),
            whole, whole,
            pl.BlockSpec((1, LANES), lambda b, h, i: (0, 0)),
        ],
        out_specs=pl.BlockSpec((1, blk, LANES), lambda b, h, i: (b, i, h)),
        out_shape=jax.ShapeDtypeStruct((batch, seq, ATTN_WIDTH), BF16),
        compiler_params=_params(("parallel", "parallel", "arbitrary")),
        name="flash_attention",
    )(lam, q3, k3, v3, g_head)
    return out.reshape(batch * seq, ATTN_WIDTH)


def _decode_kernel(pt_ref, lam_ref, qt_ref, bias_ref, biasn_ref, kn_ref, vn_ref, g_ref, *rest,
                   n_rows, lam_init):
    k_refs, v_refs = rest[:DECODE_PAGES], rest[DECODE_PAGES:2 * DECODE_PAGES]
    o_ref, m_ref, l_ref, acc_ref, s_ref = rest[2 * DECODE_PAGES:]
    step = pl.program_id(1)
    is_last = step == pl.num_programs(1) - 1
    page_rows = PAGE_SIZE * N_HEADS

    @pl.when(step == 0)
    def _():
        m_ref[...] = jnp.full_like(m_ref, NEG_INF)
        l_ref[...] = jnp.zeros_like(l_ref)
        acc_ref[...] = jnp.zeros_like(acc_ref)

    eye = (lax.broadcasted_iota(jnp.int32, (LANES, LANES), 0)
           == lax.broadcasted_iota(jnp.int32, (LANES, LANES), 1))

    def to_col(row):
        return jnp.sum(jnp.where(eye, jnp.broadcast_to(row, (LANES, LANES)), 0.0), axis=1, keepdims=True)

    def contract_rows(p, v):
        return lax.dot_general(p.astype(BF16), v, (((0,), (0,)), ((), ())), preferred_element_type=F32)

    qt = qt_ref[0]
    bias = bias_ref[...]
    m_prev = m_ref[...]
    m_new = m_prev
    for r in range(DECODE_PAGES):
        k2d = k_refs[r][...].reshape(page_rows, LANES).astype(BF16)
        s = jnp.dot(k2d, qt, preferred_element_type=F32) + bias
        s_ref[r] = s
        m_new = jnp.maximum(m_new, jnp.max(s, axis=0, keepdims=True))
    s_new = (jnp.dot(kn_ref[0], qt, preferred_element_type=F32) + biasn_ref[...]
             + jnp.where(is_last, 0.0, NEG_INF))
    m_new = jnp.maximum(m_new, jnp.max(s_new, axis=0, keepdims=True))

    alpha = jnp.exp(m_prev - m_new)
    p_new = jnp.exp(s_new - m_new)
    l = alpha * l_ref[...] + jnp.sum(p_new, axis=0, keepdims=True)
    pv = contract_rows(p_new, vn_ref[0])
    for r in range(DECODE_PAGES):
        p = jnp.exp(s_ref[r] - m_new)
        l = l + jnp.sum(p, axis=0, keepdims=True)
        pv = pv + contract_rows(p, v_refs[r][...].reshape(page_rows, LANES).astype(BF16))
    acc = to_col(alpha) * acc_ref[...] + pv
    acc_ref[...] = acc
    m_ref[...] = m_new
    l_ref[...] = l

    @pl.when(is_last)
    def _():
        o = acc / to_col(l)
        o = o[:n_rows] - lam_ref[0] * o[n_rows:2 * n_rows]
        o_ref[0] = _head_finish(o, g_ref[...], lam_init).astype(BF16)


def _decode_attention(q, k_new, v_new, cache_k, cache_v, page_table, g_head, lam, lam_init, n_dec, n_new):
    n_pages = page_table.shape[1]
    n_rows = N_HEADS * n_new
    q5 = q.reshape(n_dec, n_new, N_HEADS, 2, HEAD_DK)
    qt = jnp.einsum("bthmd,mM->bMdmht", q5, jnp.eye(2, dtype=BF16)).reshape(n_dec, LANES, 2 * n_rows)
    qt = jnp.pad(qt, ((0, 0), (0, 0), (0, LANES - 2 * n_rows)))
    col = jnp.arange(LANES)
    col_ok, col_head, col_t = col < 2 * n_rows, (col % n_rows) // n_new, col % n_new
    row = jnp.arange(PAGE_SIZE * N_HEADS)
    bias = jnp.where(col_ok[None] & (row[:, None] % N_HEADS == col_head[None]), 0.0, NEG_INF).astype(F32)
    row_n = jnp.arange(n_rows)
    bias_new = jnp.where(col_ok[None] & (row_n[:, None] % N_HEADS == col_head[None])
                         & (row_n[:, None] // N_HEADS <= col_t[None]), 0.0, NEG_INF).astype(F32)
    kn = k_new.reshape(n_dec, n_rows, LANES)
    vn = v_new.reshape(n_dec, n_rows, LANES)

    def page_spec(r):
        return pl.BlockSpec((None, PAGE_SIZE, N_HEADS, LANES),
                            lambda b, s, pt: (pt[b * n_pages + s * DECODE_PAGES + r], 0, 0, 0))

    const = lambda shape: pl.BlockSpec(shape, lambda b, s, pt: (0,) * len(shape))
    per_b = lambda shape: pl.BlockSpec(shape, lambda b, s, pt: (b,) + (0,) * (len(shape) - 1))
    pages = [page_spec(r) for r in range(DECODE_PAGES)]
    grid_spec = pltpu.PrefetchScalarGridSpec(
        num_scalar_prefetch=1,
        grid=(n_dec, n_pages // DECODE_PAGES),
        in_specs=[pl.BlockSpec(memory_space=pltpu.SMEM), per_b((1, LANES, LANES)),
                  const((PAGE_SIZE * N_HEADS, LANES)), const((n_rows, LANES)),
                  per_b((1, n_rows, LANES)), per_b((1, n_rows, LANES)), const((1, LANES))]
                 + pages + pages,
        out_specs=per_b((1, n_rows, HEAD_DV)),
        scratch_shapes=[pltpu.VMEM((1, LANES), F32), pltpu.VMEM((1, LANES), F32),
                        pltpu.VMEM((LANES, HEAD_DV), F32),
                        pltpu.VMEM((DECODE_PAGES, PAGE_SIZE * N_HEADS, LANES), F32)],
    )
    out = pl.pallas_call(
        functools.partial(_decode_kernel, n_rows=n_rows, lam_init=lam_init),
        grid_spec=grid_spec,
        out_shape=jax.ShapeDtypeStruct((n_dec, n_rows, HEAD_DV), BF16),
        compiler_params=_params(("parallel", "arbitrary")),
        name="decode_attention",
    )(page_table.reshape(-1), lam, qt, bias, bias_new, kn, vn, g_head,
      *([cache_k] * DECODE_PAGES), *([cache_v] * DECODE_PAGES))
    out = out.reshape(n_dec, N_HEADS, n_new, HEAD_DV).transpose(0, 2, 1, 3)
    return out.reshape(n_dec * n_new, ATTN_WIDTH)


def _ssm_tables(a_re, a_im, log_dt, b_re, b_im, c_re, c_im, d_skip, n_pow):
    dt = jnp.exp(log_dt)[:, None]
    mag = jnp.exp(a_re * dt)
    ab_re, ab_im = mag * jnp.cos(a_im * dt), mag * jnp.sin(a_im * dt)
    den = a_re * a_re + a_im * a_im
    f_re = ((ab_re - 1.0) * a_re + ab_im * a_im) / den
    f_im = (ab_im * a_re - (ab_re - 1.0) * a_im) / den
    bb_re = f_re[..., None] * b_re - f_im[..., None] * b_im
    bb_im = f_re[..., None] * b_im + f_im[..., None] * b_re
    eye = jnp.eye(SSM_GB, dtype=F32)

    def b_blocks(bb):
        bb = bb.reshape(SSM_NB, SSM_GB, SSM_STATE, SSM_GROUP)
        return jnp.einsum("agpc,gh->agchp", bb, eye).reshape(SSM_NB, LANES, SSM_SW)

    def c_blocks(cc):
        cc = cc.reshape(SSM_NB, SSM_GB, SSM_GROUP, SSM_STATE)
        return jnp.einsum("agcp,gh->agphc", cc, eye).reshape(SSM_NB, SSM_SW, LANES)

    bbd = jnp.concatenate([b_blocks(bb_re), b_blocks(bb_im)], axis=2).astype(BF16)
    cbd = jnp.concatenate([c_blocks(c_re), c_blocks(-c_im)], axis=1).astype(BF16)
    d = d_skip.reshape(SSM_NB, 1, LANES)
    pr, pi = ab_re[None], ab_im[None]
    while pr.shape[0] < n_pow:
        tr, ti = pr[-1:], pi[-1:]
        pr, pi = (jnp.concatenate([pr, pr * tr - pi * ti], axis=0),
                  jnp.concatenate([pi, pr * ti + pi * tr], axis=0))
    lay = lambda t: t[:n_pow].reshape(n_pow, SSM_NB, SSM_SW).transpose(1, 0, 2)
    apow = jnp.concatenate([lay(pr), lay(pi)], axis=2)
    return bbd, cbd, d, apow


def _shift_rows(x, d):
    n = x.shape[0]
    if d % SUBLANES == 0:
        return jnp.concatenate([jnp.zeros((d, x.shape[1]), x.dtype), x[:n - d]], axis=0)
    rolled = pltpu.roll(x, d, axis=0)
    row = lax.broadcasted_iota(jnp.int32, x.shape, 0)
    return jnp.where(row < d, 0.0, rolled)


def _ssm_prompt_kernel(u_ref, b_ref, c_ref, d_ref, ap_ref, y_ref, sre_ref, sim_ref, h_ref):
    chunk = pl.program_id(2)

    @pl.when(chunk == 0)
    def _():
        h_ref[...] = jnp.zeros_like(h_ref)

    u = u_ref[0]
    length = u.shape[0]
    bu = jnp.dot(u.astype(BF16), b_ref[0], preferred_element_type=F32)
    re, im = bu[:, :SSM_SW], bu[:, SSM_SW:]
    ap = ap_ref[0]
    d = 1
    while d < length:
        ar, ai = ap[d - 1:d, :SSM_SW], ap[d - 1:d, SSM_SW:]
        sr, si = _shift_rows(re, d), _shift_rows(im, d)
        re, im = re + ar * sr - ai * si, im + ar * si + ai * sr
        d *= 2
    hr, hi = h_ref[0:1, :SSM_SW], h_ref[0:1, SSM_SW:]
    pr, pi = ap[:, :SSM_SW], ap[:, SSM_SW:]
    re, im = re + pr * hr - pi * hi, im + pr * hi + pi * hr
    h_ref[0:1, :SSM_SW] = re[length - 1:length]
    h_ref[0:1, SSM_SW:] = im[length - 1:length]
    hcat = jnp.concatenate([re, im], axis=1).astype(BF16)
    y = jnp.dot(hcat, c_ref[0], preferred_element_type=F32) + d_ref[0] * u
    y_ref[0] = jax.nn.gelu(y).astype(BF16)

    @pl.when(chunk == pl.num_programs(2) - 1)
    def _():
        sre_ref[0, 0] = re[length - 1:length]
        sim_ref[0, 0] = im[length - 1:length]


def _ssm_prompt(z, tables, batch, seq):
    bbd, cbd, d, apow = tables
    length = apow.shape[1]
    z3 = z.reshape(batch, seq, IN_COLS)
    u_blk = V_END // LANES
    y, sre, sim = pl.pallas_call(
        _ssm_prompt_kernel,
        grid=(batch, SSM_NB, seq // length),
        in_specs=[
            pl.BlockSpec((1, length, LANES), lambda b, g, c: (b, c, u_blk + g)),
            pl.BlockSpec((1, LANES, 2 * SSM_SW), lambda b, g, c: (g, 0, 0)),
            pl.BlockSpec((1, 2 * SSM_SW, LANES), lambda b, g, c: (g, 0, 0)),
            pl.BlockSpec((1, 1, LANES), lambda b, g, c: (g, 0, 0)),
            pl.BlockSpec((1, length, 2 * SSM_SW), lambda b, g, c: (g, 0, 0)),
        ],
        out_specs=[
            pl.BlockSpec((1, length, LANES), lambda b, g, c: (b, c, g)),
            pl.BlockSpec((1, 1, 1, SSM_SW), lambda b, g, c: (b, g, 0, 0)),
            pl.BlockSpec((1, 1, 1, SSM_SW), lambda b, g, c: (b, g, 0, 0)),
        ],
        out_shape=[jax.ShapeDtypeStruct((batch, seq, SSM_WIDTH), BF16),
                   jax.ShapeDtypeStruct((batch, SSM_NB, 1, SSM_SW), F32),
                   jax.ShapeDtypeStruct((batch, SSM_NB, 1, SSM_SW), F32)],
        scratch_shapes=[pltpu.VMEM((SUBLANES, 2 * SSM_SW), F32)],
        compiler_params=_params(("parallel", "parallel", "arbitrary")),
        name="ssm_prompt",
    )(z3, bbd, cbd, d, apow)
    state = lambda s: s.reshape(batch, SSM_GROUPS, SSM_STATE)
    return y.reshape(batch * seq, SSM_WIDTH), state(sre), state(sim)


def _ssm_step_kernel(u_ref, b_ref, c_ref, d_ref, ap_ref, hre_ref, him_ref, y_ref, sre_ref, sim_ref):
    ar, ai = ap_ref[0, 0:1, :SSM_SW], ap_ref[0, 0:1, SSM_SW:]
    re, im = hre_ref[...], him_ref[...]
    bmat = b_ref[0]
    for t in range(u_ref.shape[0]):
        u = u_ref[t]
        u_hi = u.astype(BF16)
        u_lo = (u - u_hi.astype(F32)).astype(BF16)
        bu = (jnp.dot(u_hi, bmat, preferred_element_type=F32)
              + jnp.dot(u_lo, bmat, preferred_element_type=F32))
        re, im = (ar * re - ai * im + bu[:, :SSM_SW], ar * im + ai * re + bu[:, SSM_SW:])
        hcat = jnp.concatenate([re, im], axis=1).astype(BF16)
        y = jnp.dot(hcat, c_ref[0], preferred_element_type=F32) + d_ref[0] * u
        y_ref[t] = jax.nn.gelu(y).astype(BF16)
    sre_ref[...] = re
    sim_ref[...] = im


def _ssm_step(z, tables, h_re, h_im, batch, steps):
    bbd, cbd, d, apow = tables
    u = z[:, V_END:U_END].reshape(batch, steps, SSM_WIDTH).transpose(1, 0, 2)
    flat = lambda s: s.reshape(batch, SSM_GROUPS * SSM_STATE)
    y, sre, sim = pl.pallas_call(
        _ssm_step_kernel,
        grid=(SSM_NB,),
        in_specs=[
            pl.BlockSpec((steps, batch, LANES), lambda g: (0, 0, g)),
            pl.BlockSpec((1, LANES, 2 * SSM_SW), lambda g: (g, 0, 0)),
            pl.BlockSpec((1, 2 * SSM_SW, LANES), lambda g: (g, 0, 0)),
            pl.BlockSpec((1, 1, LANES), lambda g: (g, 0, 0)),
            pl.BlockSpec((1, SUBLANES, 2 * SSM_SW), lambda g: (g, 0, 0)),
            pl.BlockSpec((batch, SSM_SW), lambda g: (0, g)),
            pl.BlockSpec((batch, SSM_SW), lambda g: (0, g)),
        ],
        out_specs=[
            pl.BlockSpec((steps, batch, LANES), lambda g: (0, 0, g)),
            pl.BlockSpec((batch, SSM_SW), lambda g: (0, g)),
            pl.BlockSpec((batch, SSM_SW), lambda g: (0, g)),
        ],
        out_shape=[jax.ShapeDtypeStruct((steps, batch, SSM_WIDTH), BF16),
                   jax.ShapeDtypeStruct((batch, SSM_GROUPS * SSM_STATE), F32),
                   jax.ShapeDtypeStruct((batch, SSM_GROUPS * SSM_STATE), F32)],
        compiler_params=_params(("parallel",)),
        name="ssm_step",
    )(u, bbd, cbd, d, apow, flat(h_re), flat(h_im))
    y = y.transpose(1, 0, 2).reshape(batch * steps, SSM_WIDTH)
    state = lambda s: s.reshape(batch, SSM_GROUPS, SSM_STATE)
    return y, state(sre), state(sim)


def _merge_kernel(att_ref, ys_ref, za_ref, zs_ref, wup_ref, wga_ref, wgb_ref, o_ref):
    ys = ys_ref[...]
    branch_a = jnp.dot(att_ref[...], wup_ref[...], preferred_element_type=F32)
    glu_a = jnp.dot(ys, wga_ref[...], preferred_element_type=F32)
    glu_b = jnp.dot(ys, wgb_ref[...], preferred_element_type=F32)
    branch_s = glu_a * jax.nn.sigmoid(glu_b)
    merged = jax.nn.sigmoid(za_ref[...]) * branch_a + jax.nn.sigmoid(zs_ref[...]) * branch_s
    o_ref[...] = merged.astype(BF16)


def _merge(att, ys, z, w_up, w_glu, tm, tn):
    n = att.shape[0]
    nj = D_MODEL // tn
    act = pl.BlockSpec((tm, ATTN_WIDTH), lambda j, i: (i, 0))
    gate = lambda off: pl.BlockSpec((tm, tn), lambda j, i: (i, off + j))
    wcol = lambda off: pl.BlockSpec((ATTN_WIDTH, tn), lambda j, i: (0, off + j))
    return pl.pallas_call(
        _merge_kernel,
        grid=(nj, n // tm),
        in_specs=[act, act, gate(U_END // tn), gate((U_END + D_MODEL) // tn),
                  wcol(0), wcol(0), wcol(nj)],
        out_specs=pl.BlockSpec((tm, tn), lambda j, i: (i, j)),
        out_shape=jax.ShapeDtypeStruct((n, D_MODEL), BF16),
        compiler_params=_params(("parallel", "parallel")),
        name="merge",
    )(att, ys, z, z, w_up, w_glu, w_glu)


def _out_proj_kernel(x_ref, m_ref, w_ref, g_ref, x1_ref, hn_ref):
    x1 = x_ref[...] + jnp.dot(m_ref[...], w_ref[...], preferred_element_type=F32)
    x1_ref[...] = x1
    ms = jnp.mean(x1 * x1, axis=-1, keepdims=True)
    hn_ref[...] = (x1 * lax.rsqrt(ms + RMS_EPS) * g_ref[...]).astype(BF16)


def _out_proj(x, merged, w_out, g_ffn, tm):
    n = x.shape[0]
    row = pl.BlockSpec((tm, D_MODEL), lambda i: (i, 0))
    return pl.pallas_call(
        _out_proj_kernel,
        grid=(n // tm,),
        in_specs=[row, row, pl.BlockSpec((D_MODEL, D_MODEL), lambda i: (0, 0)),
                  pl.BlockSpec((1, D_MODEL), lambda i: (0, 0))],
        out_specs=[row, row],
        out_shape=[jax.ShapeDtypeStruct((n, D_MODEL), F32),
                   jax.ShapeDtypeStruct((n, D_MODEL), BF16)],
        compiler_params=_params(("parallel",)),
        name="out_proj",
    )(x, merged, w_out, g_ffn)


def _top_rows(s, k):
    n = s.shape[0]
    row = lax.broadcasted_iota(jnp.int32, s.shape, 0)
    vals, idxs = [], []
    for _ in range(k):
        best = jnp.max(s, axis=0, keepdims=True)
        pick = jnp.min(jnp.where(s == best, row, n), axis=0, keepdims=True)
        vals.append(best)
        idxs.append(pick)
        s = jnp.where(row == pick, -jnp.inf, s)
    return jnp.concatenate(vals, axis=0), jnp.concatenate(idxs, axis=0)


def _router_kernel(h_ref, wq_ref, sk_ref, a_ref, b_ref, g_ref):
    tm = h_ref.shape[0]
    k = PEER_TOPK
    q = jnp.dot(h_ref[...], wq_ref[...], preferred_element_type=F32).astype(BF16)
    half = PEER_DKEY // 2

    n_rows = k + 8 * SUBLANES
    r = lax.broadcasted_iota(jnp.int32, (n_rows, tm), 0)
    p_mid = ((r - k) >> 3) + 1
    q_mid = (r - k) & 7
    valid = (r < k) | (r >= k + 7 * SUBLANES) | ((p_mid + 1) * (q_mid + 1) <= k)

    def cand_rows(first, second):
        parts = [jnp.broadcast_to(first[0:1], (k, tm)) if second is None else first[0:1] + second[0:k]]
        for p in range(1, 8):
            parts.append(jnp.broadcast_to(first[p:p + 1], (SUBLANES, tm)) if second is None
                         else first[p:p + 1] + second[0:SUBLANES])
        parts.append(first[8:16] if second is None else first[8:16] + second[0:1])
        return jnp.concatenate(parts, axis=0)

    def second_rows(second):
        parts = [second[0:k]] + [second[0:SUBLANES]] * 7
        parts.append(jnp.broadcast_to(second[0:1], (SUBLANES, tm)))
        return jnp.concatenate(parts, axis=0)

    for h in range(PEER_HEADS):
        tops = []
        for m in range(2):
            c = 2 * h + m
            s = _nt_dot(sk_ref[m], q[:, c * half:(c + 1) * half])
            tops.append(_top_rows(s, k))
        (v1, i1), (v2, i2) = tops
        cand = jnp.where(valid, cand_rows(v1, v2), -jnp.inf)
        first_idx = cand_rows(i1, None)
        second_idx = second_rows(i2)
        fv, frow = _top_rows(cand, k)
        sel_a, sel_b = [], []
        for j in range(k):
            hit = r == frow[j:j + 1]
            sel_a.append(jnp.max(jnp.where(hit, first_idx, -1), axis=0, keepdims=True))
            sel_b.append(jnp.max(jnp.where(hit, second_idx, -1), axis=0, keepdims=True))
        e = jnp.exp(fv - fv[0:1])
        gate = e / jnp.sum(e, axis=0, keepdims=True)
        a_ref[h * k:(h + 1) * k, :] = jnp.concatenate(sel_a, axis=0)
        b_ref[h * k:(h + 1) * k, :] = jnp.concatenate(sel_b, axis=0)
        g_ref[h * k:(h + 1) * k, :] = gate


def _router(hn, w_query, sub_keys, tm):
    n = hn.shape[0]
    slots = PEER_HEADS * PEER_TOPK
    out = pl.BlockSpec((slots, tm), lambda i: (0, i))
    return pl.pallas_call(
        _router_kernel,
        grid=(n // tm,),
        in_specs=[pl.BlockSpec((tm, D_MODEL), lambda i: (i, 0)),
                  pl.BlockSpec((D_MODEL, PEER_HEADS * PEER_DKEY), lambda i: (0, 0)),
                  pl.BlockSpec((2, PEER_KEYS, PEER_DKEY // 2), lambda i: (0, 0, 0))],
        out_specs=[out, out, out],
        out_shape=[jax.ShapeDtypeStruct((slots, n), jnp.int32),
                   jax.ShapeDtypeStruct((slots, n), jnp.int32),
                   jax.ShapeDtypeStruct((slots, n), F32)],
        compiler_params=_params(("parallel",)),
        name="router",
    )(hn, w_query, sub_keys)


def _wbuild_kernel(a_ref, b_ref, g_ref, w_ref):
    tm = a_ref.shape[0]
    idx = lax.broadcasted_iota(jnp.int32, (PEER_KEYS, a_ref.shape[1]), 0)

    def body(i, carry):
        base = pl.multiple_of(i * WBUILD_UNROLL, WBUILD_UNROLL)
        a_rows = a_ref[pl.ds(base, WBUILD_UNROLL), :]
        b_rows = b_ref[pl.ds(base, WBUILD_UNROLL), :]
        g_rows = g_ref[pl.ds(base, WBUILD_UNROLL), :]
        for t in range(WBUILD_UNROLL):
            first = jnp.where(a_rows[t:t + 1] == idx, g_rows[t:t + 1], 0.0).astype(BF16)
            second = jnp.where(b_rows[t:t + 1] == idx, 1.0, 0.0).astype(BF16)
            w_ref[base + t] = _nt_dot(first, second)
        return carry

    lax.fori_loop(0, tm // WBUILD_UNROLL, body, 0)


def _wbuild(a, b, g, tm):
    n, slots = a.shape
    row = pl.BlockSpec((tm, slots), lambda i: (i, 0))
    return pl.pallas_call(
        _wbuild_kernel,
        grid=(n // tm,),
        in_specs=[row, row, row],
        out_specs=pl.BlockSpec((tm, PEER_KEYS, PEER_KEYS), lambda i: (i, 0, 0)),
        out_shape=jax.ShapeDtypeStruct((n, PEER_KEYS, PEER_KEYS), F32),
        compiler_params=_params(("parallel",)),
        name="wbuild",
    )(a, b, g)


def _experts_kernel(h_ref, ut_ref, v_ref, w_ref, o_ref):
    @pl.when(pl.program_id(1) == 0)
    def _():
        o_ref[...] = jnp.zeros_like(o_ref)

    s = jnp.dot(h_ref[...], ut_ref[...], preferred_element_type=F32)
    w = jnp.concatenate([w_ref[:, r, :] for r in range(w_ref.shape[1])], axis=1)
    c = (jax.nn.gelu(s) * w).astype(BF16)
    o_ref[...] += jnp.dot(c, v_ref[...], preferred_element_type=F32)


def _experts(hn, u_t, v, w, tm):
    n = hn.shape[0]
    keys_per_tile = EXPERT_TILE // PEER_KEYS
    return pl.pallas_call(
        _experts_kernel,
        grid=(n // tm, PEER_EXPERTS // EXPERT_TILE),
        in_specs=[pl.BlockSpec((tm, D_MODEL), lambda i, j: (i, 0)),
                  pl.BlockSpec((D_MODEL, EXPERT_TILE), lambda i, j: (0, j)),
                  pl.BlockSpec((EXPERT_TILE, D_MODEL), lambda i, j: (j, 0)),
                  pl.BlockSpec((tm, keys_per_tile, PEER_KEYS), lambda i, j: (i, j, 0))],
        out_specs=pl.BlockSpec((tm, D_MODEL), lambda i, j: (i, 0)),
        out_shape=jax.ShapeDtypeStruct((n, D_MODEL), F32),
        compiler_params=_params(("parallel", "arbitrary")),
        name="experts",
    )(hn, u_t, v, w)


def _ple_kernel(x1_ref, peer_ref, p_ref, wp_ref, wg_ref, g_ref, o_ref):
    x2 = x1_ref[...] + peer_ref[...]
    ms = jnp.mean(x2 * x2, axis=-1, keepdims=True)
    hn = (x2 * lax.rsqrt(ms + RMS_EPS) * g_ref[...]).astype(BF16)
    emb = jnp.dot(p_ref[...].astype(BF16), wp_ref[...], preferred_element_type=F32)
    gate = jax.nn.sigmoid(jnp.dot(hn, wg_ref[...], preferred_element_type=F32))
    o_ref[...] = x2 + emb * gate


def _ple(x1, peer, p, w_ple, w_gate, g_ple, tm):
    n = x1.shape[0]
    row = pl.BlockSpec((tm, D_MODEL), lambda i: (i, 0))
    return pl.pallas_call(
        _ple_kernel,
        grid=(n // tm,),
        in_specs=[row, row, pl.BlockSpec((tm, PLE_DIM), lambda i: (i, 0)),
                  pl.BlockSpec((PLE_DIM, D_MODEL), lambda i: (0, 0)),
                  pl.BlockSpec((D_MODEL, D_MODEL), lambda i: (0, 0)),
                  pl.BlockSpec((1, D_MODEL), lambda i: (0, 0))],
        out_specs=row,
        out_shape=jax.ShapeDtypeStruct((n, D_MODEL), F32),
        compiler_params=_params(("parallel",)),
        name="ple",
    )(x1, peer, p, w_ple, w_gate, g_ple)


def _tile(n, pref):
    return pref if n % pref == 0 else n


def _layer(x, p, pos_rows, attention, ssm, wts):
    n = x.shape[0]
    z = _in_proj(x, wts["g_mix"], wts["w_in"], _tile(n, 512), 1024)
    cos, sin_lo, sin_hi = _rope_tables(pos_rows)
    q, k_f32, k_bf, v_bf = _qkv_prep(z, wts["g_q"], wts["g_k"], wts["bd"], cos, sin_lo, sin_hi,
                                     _tile(cos.shape[0], 256))
    att = attention(q, k_bf, v_bf)
    ys, s_re, s_im = ssm(z)
    merged = _merge(att, ys, z, wts["w_attn_up"], wts["w_glu"], _tile(n, 512), 512)
    x1, hn = _out_proj(x, merged, wts["w_out"], wts["g_ffn"], _tile(n, 256))
    a_t, b_t, g_t = _router(hn, wts["peer_w_query"], wts["peer_sub_keys"], 128)
    w = _wbuild(a_t.T, b_t.T, g_t.T, _tile(n, 64))
    peer = _experts(hn, wts["peer_u_t"], wts["peer_v"], w, _tile(n, 512))
    y = _ple(x1, peer, p, wts["w_ple"], wts["w_ple_gate"], wts["g_ple"], _tile(n, 256))
    k_new = k_f32.reshape(n, N_HEADS, 2 * HEAD_DK)
    v_new = z[:, 2 * QK_COLS:V_END].reshape(n, N_HEADS, HEAD_DV)
    return y, k_new, v_new, s_re, s_im


def kernel(x_prompt, x_sample, cache_k, cache_v, state_ssm_re, state_ssm_im, page_table, p_prompt, p_sample, g_mix, w_in, g_q, g_k, lambda_q, lambda_k, g_head, w_attn_up, ssm_a_re, ssm_a_im, ssm_log_dt, ssm_b_re, ssm_b_im, ssm_c_re, ssm_c_im, ssm_d, w_glu, w_out, g_ffn, peer_w_query, peer_sub_keys, peer_u, peer_v, g_ple, w_ple, w_ple_gate):
    depth = w_in.shape[0]
    assert depth == 1
    batch, seq, _ = x_prompt.shape
    n_dec, n_new, _ = x_sample.shape
    past_len = page_table.shape[1] * PAGE_SIZE
    i = 0
    lam_init = 0.8 - 0.6 * math.exp(-0.3 * i)

    row = lambda t: t.reshape(1, -1)
    bd = jnp.kron(jnp.eye(2, dtype=F32), jnp.ones((HEAD_DK, HEAD_DK), F32)).astype(BF16)
    wts = {
        "g_mix": row(g_mix[i]), "w_in": w_in[i].astype(BF16),
        "g_q": row(g_q[i]), "g_k": row(g_k[i]), "bd": bd,
        "w_attn_up": w_attn_up[i].astype(BF16), "w_glu": w_glu[i].astype(BF16),
        "w_out": w_out[i].astype(BF16), "g_ffn": row(g_ffn[i]),
        "peer_w_query": peer_w_query[i].astype(BF16), "peer_sub_keys": peer_sub_keys[i].astype(BF16),
        "peer_u_t": peer_u[i].T.astype(BF16), "peer_v": peer_v[i].astype(BF16),
        "g_ple": row(g_ple[i]), "w_ple": w_ple[i].astype(BF16), "w_ple_gate": w_ple_gate[i].astype(BF16),
    }
    g_head2 = row(g_head[i])
    lq, lk = lambda_q[i].astype(F32), lambda_k[i].astype(F32)
    lam = (jnp.exp(jnp.sum(lq[0] * lk[0])) - jnp.exp(jnp.sum(lq[1] * lk[1])) + lam_init).reshape(1)
    ssm_args = (ssm_a_re[i], ssm_a_im[i], ssm_log_dt[i], ssm_b_re[i], ssm_b_im[i],
                ssm_c_re[i], ssm_c_im[i], ssm_d[i])
    tables_prompt = _ssm_tables(*ssm_args, SSM_CHUNK)
    tables_step = _ssm_tables(*ssm_args, SUBLANES)

    pos_prompt = jnp.arange(seq, dtype=jnp.int32)
    y_p, k_p, v_p, sre_p, sim_p = _layer(
        x_prompt.reshape(batch * seq, D_MODEL), p_prompt[i].reshape(batch * seq, PLE_DIM), pos_prompt,
        lambda q, k, v: _flash_attention(q, k, v, g_head2, lam, lam_init, batch, seq, FLASH_BLOCK),
        lambda z: _ssm_prompt(z, tables_prompt, batch, seq),
        wts)

    pos_sample = jnp.tile(past_len + jnp.arange(n_new, dtype=jnp.int32), n_dec)
    y_s, k_s, v_s, sre_s, sim_s = _layer(
        x_sample.reshape(n_dec * n_new, D_MODEL), p_sample[i].reshape(n_dec * n_new, PLE_DIM), pos_sample,
        lambda q, k, v: _decode_attention(q, k, v, cache_k[i], cache_v[i], page_table, g_head2, lam,
                                          lam_init, n_dec, n_new),
        lambda z: _ssm_step(z, tables_step, state_ssm_re[i], state_ssm_im[i], n_dec, n_new),
        wts)

    lead = lambda t, *shape: t.reshape(1, *shape)
    return (y_p.reshape(batch, seq, D_MODEL), y_s.reshape(n_dec, n_new, D_MODEL),
            lead(k_p, batch, seq, N_HEADS, 2 * HEAD_DK), lead(v_p, batch, seq, N_HEADS, HEAD_DV),
            lead(sre_p, batch, SSM_GROUPS, SSM_STATE), lead(sim_p, batch, SSM_GROUPS, SSM_STATE),
            lead(k_s, n_dec, n_new, N_HEADS, 2 * HEAD_DK), lead(v_s, n_dec, n_new, N_HEADS, HEAD_DV),
            lead(sre_s, n_dec, SSM_GROUPS, SSM_STATE), lead(sim_s, n_dec, SSM_GROUPS, SSM_STATE))
```

```python
import functools
import math

import jax
import jax.numpy as jnp
from jax import lax
from jax.experimental import pallas as pl
from jax.experimental.pallas import tpu as pltpu

F32 = jnp.float32
BF16 = jnp.bfloat16

D_MODEL = 2048
PAGE_SIZE = 128
N_HEADS = 8
HEAD_DK = 64
HEAD_DV = 2 * HEAD_DK
QK_COLS = N_HEADS * 2 * HEAD_DK
ATTN_WIDTH = N_HEADS * HEAD_DV
ROPE_DIMS = HEAD_DK // 4
ROPE_THETA = 500000.0
NEG_INF = -1e30
SSM_WIDTH = D_MODEL // 2
SSM_GROUP = 16
SSM_GROUPS = SSM_WIDTH // SSM_GROUP
SSM_STATE = 64
PEER_HEADS = 8
PEER_KEYS = 128
PEER_EXPERTS = PEER_KEYS * PEER_KEYS
PEER_DKEY = 256
PEER_TOPK = 16
PLE_DIM = 256
RMS_EPS = 1e-6
V_END = 2 * QK_COLS + ATTN_WIDTH
U_END = V_END + SSM_WIDTH
IN_COLS = U_END + 2 * D_MODEL

LANES = 128
SUBLANES = 8
VMEM_LIMIT = 52 * 1024 * 1024
SSM_GB = LANES // SSM_GROUP
SSM_NB = SSM_GROUPS // SSM_GB
SSM_SW = SSM_GB * SSM_STATE
SSM_CHUNK = 128
SCAN_STEPS = (1, 2, 4)
DECODE_PAGES = 8
FLASH_BLOCK = 512
WBUILD_UNROLL = 8
EXPERT_TILE = 1024


def _params(sem):
    return pltpu.CompilerParams(dimension_semantics=sem, vmem_limit_bytes=VMEM_LIMIT)


def _nt_dot(a, b):
    return lax.dot_general(a, b, (((1,), (1,)), ((), ())), preferred_element_type=F32)


def _in_proj_kernel(x_ref, g_ref, w_ref, o_ref, h_ref):
    @pl.when(pl.program_id(1) == 0)
    def _():
        x = x_ref[...]
        ms = jnp.mean(x * x, axis=-1, keepdims=True)
        h_ref[...] = (x * lax.rsqrt(ms + RMS_EPS) * g_ref[...]).astype(BF16)

    o_ref[...] = jnp.dot(h_ref[...], w_ref[...], preferred_element_type=F32)


def _in_proj(x, g, w_bf, tm, tn):
    n, d = x.shape
    cols = w_bf.shape[1]
    return pl.pallas_call(
        _in_proj_kernel,
        grid=(n // tm, cols // tn),
        in_specs=[
            pl.BlockSpec((tm, d), lambda i, j: (i, 0)),
            pl.BlockSpec((1, d), lambda i, j: (0, 0)),
            pl.BlockSpec((d, tn), lambda i, j: (0, j)),
        ],
        out_specs=pl.BlockSpec((tm, tn), lambda i, j: (i, j)),
        out_shape=jax.ShapeDtypeStruct((n, cols), F32),
        scratch_shapes=[pltpu.VMEM((tm, d), BF16)],
        compiler_params=_params(("parallel", "arbitrary")),
        name="in_proj",
    )(x, g, w_bf)


def _qkv_prep_kernel(zq_ref, zk_ref, zv_ref, gq_ref, gk_ref, bd_ref, c_ref, s1_ref, s2_ref,
                     q_ref, kf_ref, kb_ref, vb_ref):
    bd = bd_ref[...]
    cos, sin_lo, sin_hi = c_ref[...], s1_ref[...], s2_ref[...]

    def norm_rope(x, g):
        x2 = x * x
        hi = x2.astype(BF16)
        lo = (x2 - hi.astype(F32)).astype(BF16)
        ss = (jnp.dot(hi, bd, preferred_element_type=F32)
              + jnp.dot(lo, bd, preferred_element_type=F32))
        xn = x * lax.rsqrt(ss * (1.0 / HEAD_DK) + RMS_EPS) * g
        fwd = pltpu.roll(xn, LANES - ROPE_DIMS // 2, axis=1)
        bwd = pltpu.roll(xn, ROPE_DIMS // 2, axis=1)
        return xn * cos + fwd * sin_lo + bwd * sin_hi

    for c in range(QK_COLS // LANES):
        sl = slice(c * LANES, (c + 1) * LANES)
        q = norm_rope(zq_ref[:, sl], gq_ref[...])
        q_ref[:, sl] = (q * (HEAD_DK ** -0.5)).astype(BF16)
        k = norm_rope(zk_ref[:, sl], gk_ref[...])
        kf_ref[:, sl] = k
        kb_ref[:, sl] = k.astype(BF16)
    vb_ref[...] = zv_ref[...].astype(BF16)


def _qkv_prep(z, gq, gk, bd, cos, sin_lo, sin_hi, tm):
    n = z.shape[0]
    nt = cos.shape[0] // tm
    row = lambda c: pl.BlockSpec((tm, QK_COLS), lambda i: (i, c))
    vec = pl.BlockSpec((1, LANES), lambda i: (0, 0))
    tab = pl.BlockSpec((tm, LANES), lambda i: (i % nt, 0))
    out = pl.BlockSpec((tm, QK_COLS), lambda i: (i, 0))
    return pl.pallas_call(
        _qkv_prep_kernel,
        grid=(n // tm,),
        in_specs=[row(0), row(1), row(2), vec, vec,
                  pl.BlockSpec((LANES, LANES), lambda i: (0, 0)), tab, tab, tab],
        out_specs=[out, out, out, out],
        out_shape=[jax.ShapeDtypeStruct((n, QK_COLS), BF16),
                   jax.ShapeDtypeStruct((n, QK_COLS), F32),
                   jax.ShapeDtypeStruct((n, QK_COLS), BF16),
                   jax.ShapeDtypeStruct((n, ATTN_WIDTH), BF16)],
        compiler_params=_params(("parallel",)),
        name="qkv_prep",
    )(z, z, z, gq, gk, bd, cos, sin_lo, sin_hi)


def _rope_tables(pos):
    half = ROPE_DIMS // 2
    inv_freq = ROPE_THETA ** (-jnp.arange(half, dtype=F32) / half)
    ang = pos.astype(F32)[:, None] * inv_freq[None, :]
    cos, sin = jnp.cos(ang), jnp.sin(ang)
    ones = jnp.ones((pos.shape[0], HEAD_DK - ROPE_DIMS), F32)
    zeros8 = jnp.zeros_like(sin)
    zeros = jnp.zeros_like(ones)
    c = jnp.concatenate([cos, cos, ones], axis=1)
    s_lo = jnp.concatenate([-sin, zeros8, zeros], axis=1)
    s_hi = jnp.concatenate([zeros8, sin, zeros], axis=1)
    two = lambda a: jnp.concatenate([a, a], axis=1)
    return two(c), two(s_lo), two(s_hi)


def _head_finish(o, g, lam_init):
    ms = jnp.mean(o * o, axis=-1, keepdims=True)
    return o * lax.rsqrt(ms + RMS_EPS) * g * (1.0 - lam_init)


def _flash_kernel(lam_ref, q_ref, k_ref, v_ref, g_ref, o_ref, *, blk, lam_init):
    qi = pl.program_id(2)
    q = q_ref[0]
    lane = lax.broadcasted_iota(jnp.int32, q.shape, 1)
    zero = jnp.zeros_like(q)
    qq = jnp.concatenate([jnp.where(lane < HEAD_DK, q, zero),
                          jnp.where(lane >= HEAD_DK, q, zero)], axis=0)

    def step(ki, carry, masked):
        m, l, acc = carry
        start = pl.multiple_of(ki * blk, blk)
        k = k_ref[0, pl.ds(start, blk), :]
        v = v_ref[0, pl.ds(start, blk), :]
        s = _nt_dot(qq, k)
        if masked:
            r = lax.broadcasted_iota(jnp.int32, s.shape, 0)
            c = lax.broadcasted_iota(jnp.int32, s.shape, 1)
            s = jnp.where(jnp.where(r >= blk, r - blk, r) >= c, s, NEG_INF)
        m_new = jnp.maximum(m, jnp.max(s, axis=1, keepdims=True))
        alpha = jnp.exp(m - m_new)
        p = jnp.exp(s - m_new)
        l = alpha * l + jnp.sum(p, axis=1, keepdims=True)
        acc = alpha * acc + jnp.dot(p.astype(BF16), v, preferred_element_type=F32)
        return m_new, l, acc

    init = (jnp.full((2 * blk, 1), NEG_INF, F32), jnp.zeros((2 * blk, 1), F32),
            jnp.zeros((2 * blk, HEAD_DV), F32))
    carry = lax.fori_loop(0, qi, lambda ki, c: step(ki, c, False), init)
    _, l, acc = step(qi, carry, True)
    o = acc / l
    o = o[:blk] - lam_ref[0] * o[blk:]
    o_ref[0] = _head_finish(o, g_ref[...], lam_init).astype(BF16)


def _flash_attention(q, k, v, g_head, lam, lam_init, batch, seq, blk):
    q3, k3, v3 = (t.reshape(batch, seq, QK_COLS) for t in (q, k, v))
    whole = pl.BlockSpec((1, seq, LANES), lambda b, h, i: (b, 0, h))
    out = pl.pallas_call(
        functools.partial(_flash_kernel, blk=blk, lam_init=lam_init),
        grid=(batch, N_HEADS, seq // blk),
        in_specs=[
            pl.BlockSpec(memory_space=pltpu.SMEM),
            pl.BlockSpec((1, blk, LANES), lambda b, h, i: (b, i, h)),
            whole, whole,
            pl.BlockSpec((1, LANES), lambda b, h, i: (0, 0)),
        ],
        out_specs=pl.BlockSpec((1, blk, LANES), lambda b, h, i: (b, i, h)),
        out_shape=jax.ShapeDtypeStruct((batch, seq, ATTN_WIDTH), BF16),
        compiler_params=_params(("parallel", "parallel", "arbitrary")),
        name="flash_attention",
    )(lam, q3, k3, v3, g_head)
    return out.reshape(batch * seq, ATTN_WIDTH)


def _decode_kernel(pt_ref, lam_ref, qt_ref, bias_ref, biasn_ref, kn_ref, vn_ref, g_ref, *rest,
                   n_rows, lam_init):
    k_refs, v_refs = rest[:DECODE_PAGES], rest[DECODE_PAGES:2 * DECODE_PAGES]
    o_ref, m_ref, l_ref, acc_ref, s_ref = rest[2 * DECODE_PAGES:]
    step = pl.program_id(1)
    is_last = step == pl.num_programs(1) - 1
    page_rows = PAGE_SIZE * N_HEADS

    @pl.when(step == 0)
    def _():
        m_ref[...] = jnp.full_like(m_ref, NEG_INF)
        l_ref[...] = jnp.zeros_like(l_ref)
        acc_ref[...] = jnp.zeros_like(acc_ref)

    eye = (lax.broadcasted_iota(jnp.int32, (LANES, LANES), 0)
           == lax.broadcasted_iota(jnp.int32, (LANES, LANES), 1))

    def to_col(row):
        return jnp.sum(jnp.where(eye, jnp.broadcast_to(row, (LANES, LANES)), 0.0), axis=1, keepdims=True)

    def contract_rows(p, v):
        return lax.dot_general(p.astype(BF16), v, (((0,), (0,)), ((), ())), preferred_element_type=F32)

    qt = qt_ref[0]
    bias = bias_ref[...]
    m_prev = m_ref[...]
    m_new = m_prev
    for r in range(DECODE_PAGES):
        k2d = k_refs[r][...].reshape(page_rows, LANES).astype(BF16)
        s = jnp.dot(k2d, qt, preferred_element_type=F32) + bias
        s_ref[r] = s
        m_new = jnp.maximum(m_new, jnp.max(s, axis=0, keepdims=True))
    s_new = (jnp.dot(kn_ref[0], qt, preferred_element_type=F32) + biasn_ref[...]
             + jnp.where(is_last, 0.0, NEG_INF))
    m_new = jnp.maximum(m_new, jnp.max(s_new, axis=0, keepdims=True))

    alpha = jnp.exp(m_prev - m_new)
    p_new = jnp.exp(s_new - m_new)
    l = alpha * l_ref[...] + jnp.sum(p_new, axis=0, keepdims=True)
    pv = contract_rows(p_new, vn_ref[0])
    for r in range(DECODE_PAGES):
        p = jnp.exp(s_ref[r] - m_new)
        l = l + jnp.sum(p, axis=0, keepdims=True)
        pv = pv + contract_rows(p, v_refs[r][...].reshape(page_rows, LANES).astype(BF16))
    acc = to_col(alpha) * acc_ref[...] + pv
    acc_ref[...] = acc
    m_ref[...] = m_new
    l_ref[...] = l

    @pl.when(is_last)
    def _():
        o = acc / to_col(l)
        o = o[:n_rows] - lam_ref[0] * o[n_rows:2 * n_rows]
        o_ref[0] = _head_finish(o, g_ref[...], lam_init).astype(BF16)


def _decode_attention(q, k_new, v_new, cache_k, cache_v, page_table, g_head, lam, lam_init, n_dec, n_new):
    n_pages = page_table.shape[1]
    n_rows = N_HEADS * n_new
    q5 = q.reshape(n_dec, n_new, N_HEADS, 2, HEAD_DK)
    qt = jnp.einsum("bthmd,mM->bMdmht", q5, jnp.eye(2, dtype=BF16)).reshape(n_dec, LANES, 2 * n_rows)
    qt = jnp.pad(qt, ((0, 0), (0, 0), (0, LANES - 2 * n_rows)))
    col = jnp.arange(LANES)
    col_ok, col_head, col_t = col < 2 * n_rows, (col % n_rows) // n_new, col % n_new
    row = jnp.arange(PAGE_SIZE * N_HEADS)
    bias = jnp.where(col_ok[None] & (row[:, None] % N_HEADS == col_head[None]), 0.0, NEG_INF).astype(F32)
    row_n = jnp.arange(n_rows)
    bias_new = jnp.where(col_ok[None] & (row_n[:, None] % N_HEADS == col_head[None])
                         & (row_n[:, None] // N_HEADS <= col_t[None]), 0.0, NEG_INF).astype(F32)
    kn = k_new.reshape(n_dec, n_rows, LANES)
    vn = v_new.reshape(n_dec, n_rows, LANES)

    def page_spec(r):
        return pl.BlockSpec((None, PAGE_SIZE, N_HEADS, LANES),
                            lambda b, s, pt: (pt[b * n_pages + s * DECODE_PAGES + r], 0, 0, 0))

    const = lambda shape: pl.BlockSpec(shape, lambda b, s, pt: (0,) * len(shape))
    per_b = lambda shape: pl.BlockSpec(shape, lambda b, s, pt: (b,) + (0,) * (len(shape) - 1))
    pages = [page_spec(r) for r in range(DECODE_PAGES)]
    grid_spec = pltpu.PrefetchScalarGridSpec(
        num_scalar_prefetch=1,
        grid=(n_dec, n_pages // DECODE_PAGES),
        in_specs=[pl.BlockSpec(memory_space=pltpu.SMEM), per_b((1, LANES, LANES)),
                  const((PAGE_SIZE * N_HEADS, LANES)), const((n_rows, LANES)),
                  per_b((1, n_rows, LANES)), per_b((1, n_rows, LANES)), const((1, LANES))]
                 + pages + pages,
        out_specs=per_b((1, n_rows, HEAD_DV)),
        scratch_shapes=[pltpu.VMEM((1, LANES), F32), pltpu.VMEM((1, LANES), F32),
                        pltpu.VMEM((LANES, HEAD_DV), F32),
                        pltpu.VMEM((DECODE_PAGES, PAGE_SIZE * N_HEADS, LANES), F32)],
    )
    out = pl.pallas_call(
        functools.partial(_decode_kernel, n_rows=n_rows, lam_init=lam_init),
        grid_spec=grid_spec,
        out_shape=jax.ShapeDtypeStruct((n_dec, n_rows, HEAD_DV), BF16),
        compiler_params=_params(("parallel", "arbitrary")),
        name="decode_attention",
    )(page_table.reshape(-1), lam, qt, bias, bias_new, kn, vn, g_head,
      *([cache_k] * DECODE_PAGES), *([cache_v] * DECODE_PAGES))
    out = out.reshape(n_dec, N_HEADS, n_new, HEAD_DV).transpose(0, 2, 1, 3)
    return out.reshape(n_dec * n_new, ATTN_WIDTH)


def _ssm_tables(a_re, a_im, log_dt, b_re, b_im, c_re, c_im, d_skip):
    dt = jnp.exp(log_dt)[:, None]
    mag = jnp.exp(a_re * dt)
    ab_re, ab_im = mag * jnp.cos(a_im * dt), mag * jnp.sin(a_im * dt)
    den = a_re * a_re + a_im * a_im
    f_re = ((ab_re - 1.0) * a_re + ab_im * a_im) / den
    f_im = (ab_im * a_re - (ab_re - 1.0) * a_im) / den
    bb_re = f_re[..., None] * b_re - f_im[..., None] * b_im
    bb_im = f_re[..., None] * b_im + f_im[..., None] * b_re
    eye = jnp.eye(SSM_GB, dtype=F32)

    def b_blocks(bb):
        bb = bb.reshape(SSM_NB, SSM_GB, SSM_STATE, SSM_GROUP)
        return jnp.einsum("agpc,gh->agchp", bb, eye).reshape(SSM_NB, LANES, SSM_SW)

    def c_blocks(cc):
        cc = cc.reshape(SSM_NB, SSM_GB, SSM_GROUP, SSM_STATE)
        return jnp.einsum("agcp,gh->agphc", cc, eye).reshape(SSM_NB, SSM_SW, LANES)

    bbd = jnp.concatenate([b_blocks(bb_re), b_blocks(bb_im)], axis=2).astype(BF16)
    cbd = jnp.concatenate([c_blocks(c_re), c_blocks(-c_im)], axis=1).astype(BF16)
    d = d_skip.reshape(SSM_NB, 1, LANES)
    pr, pi = ab_re[None], ab_im[None]
    while pr.shape[0] < SUBLANES:
        tr, ti = pr[-1:], pi[-1:]
        pr, pi = (jnp.concatenate([pr, pr * tr - pi * ti], axis=0),
                  jnp.concatenate([pi, pr * ti + pi * tr], axis=0))
    lay = lambda t: t.reshape(t.shape[0], SSM_NB, SSM_SW).transpose(1, 0, 2)
    apow = jnp.concatenate([lay(pr), lay(pi)], axis=2)
    rows = jnp.arange(SUBLANES)[None, :, None]
    steps = [jnp.where(rows >= dd, apow[:, dd - 1:dd, :], 0.0) for dd in SCAN_STEPS]
    amask = jnp.stack(steps, axis=1)
    return bbd, cbd, d, apow, amask


def _ssm_prompt_kernel(u_ref, b_ref, c_ref, d_ref, ap_ref, am_ref, y_ref, sre_ref, sim_ref, h_ref):
    chunk = pl.program_id(2)

    @pl.when(chunk == 0)
    def _():
        h_ref[...] = jnp.zeros_like(h_ref)

    u = u_ref[0]
    length = u.shape[0]
    bu = jnp.dot(u.astype(BF16), b_ref[0], preferred_element_type=F32)
    pr, pi = ap_ref[0, :, :SSM_SW], ap_ref[0, :, SSM_SW:]
    cr, ci = h_ref[0:1, :SSM_SW], h_ref[0:1, SSM_SW:]
    out_re, out_im = [], []
    for j in range(length // SUBLANES):
        rows = slice(j * SUBLANES, (j + 1) * SUBLANES)
        re, im = bu[rows, :SSM_SW], bu[rows, SSM_SW:]
        for si, dd in enumerate(SCAN_STEPS):
            mr, mi = am_ref[0, si, :, :SSM_SW], am_ref[0, si, :, SSM_SW:]
            sr, s_i = pltpu.roll(re, dd, axis=0), pltpu.roll(im, dd, axis=0)
            re, im = re + mr * sr - mi * s_i, im + mr * s_i + mi * sr
        re, im = re + pr * cr - pi * ci, im + pr * ci + pi * cr
        cr, ci = re[SUBLANES - 1:SUBLANES], im[SUBLANES - 1:SUBLANES]
        out_re.append(re)
        out_im.append(im)
    h_ref[0:1, :SSM_SW] = cr
    h_ref[0:1, SSM_SW:] = ci
    hcat = jnp.concatenate([jnp.concatenate(out_re, axis=0), jnp.concatenate(out_im, axis=0)],
                           axis=1).astype(BF16)
    y = jnp.dot(hcat, c_ref[0], preferred_element_type=F32) + d_ref[0] * u
    y_ref[0] = jax.nn.gelu(y).astype(BF16)

    @pl.when(chunk == pl.num_programs(2) - 1)
    def _():
        sre_ref[0, 0] = cr
        sim_ref[0, 0] = ci


def _ssm_prompt(z, tables, batch, seq):
    bbd, cbd, d, apow, amask = tables
    length = SSM_CHUNK
    z3 = z.reshape(batch, seq, IN_COLS)
    u_blk = V_END // LANES
    y, sre, sim = pl.pallas_call(
        _ssm_prompt_kernel,
        grid=(batch, SSM_NB, seq // length),
        in_specs=[
            pl.BlockSpec((1, length, LANES), lambda b, g, c: (b, c, u_blk + g)),
            pl.BlockSpec((1, LANES, 2 * SSM_SW), lambda b, g, c: (g, 0, 0)),
            pl.BlockSpec((1, 2 * SSM_SW, LANES), lambda b, g, c: (g, 0, 0)),
            pl.BlockSpec((1, 1, LANES), lambda b, g, c: (g, 0, 0)),
            pl.BlockSpec((1, SUBLANES, 2 * SSM_SW), lambda b, g, c: (g, 0, 0)),
            pl.BlockSpec((1, len(SCAN_STEPS), SUBLANES, 2 * SSM_SW), lambda b, g, c: (g, 0, 0, 0)),
        ],
        out_specs=[
            pl.BlockSpec((1, length, LANES), lambda b, g, c: (b, c, g)),
            pl.BlockSpec((1, 1, 1, SSM_SW), lambda b, g, c: (b, g, 0, 0)),
            pl.BlockSpec((1, 1, 1, SSM_SW), lambda b, g, c: (b, g, 0, 0)),
        ],
        out_shape=[jax.ShapeDtypeStruct((batch, seq, SSM_WIDTH), BF16),
                   jax.ShapeDtypeStruct((batch, SSM_NB, 1, SSM_SW), F32),
                   jax.ShapeDtypeStruct((batch, SSM_NB, 1, SSM_SW), F32)],
        scratch_shapes=[pltpu.VMEM((SUBLANES, 2 * SSM_SW), F32)],
        compiler_params=_params(("parallel", "parallel", "arbitrary")),
        name="ssm_prompt",
    )(z3, bbd, cbd, d, apow, amask)
    state = lambda s: s.reshape(batch, SSM_GROUPS, SSM_STATE)
    return y.reshape(batch * seq, SSM_WIDTH), state(sre), state(sim)


def _ssm_step_kernel(u_ref, b_ref, c_ref, d_ref, ap_ref, hre_ref, him_ref, y_ref, sre_ref, sim_ref):
    ar, ai = ap_ref[0, 0:1, :SSM_SW], ap_ref[0, 0:1, SSM_SW:]
    re, im = hre_ref[...], him_ref[...]
    bmat = b_ref[0]
    for t in range(u_ref.shape[0]):
        u = u_ref[t]
        u_hi = u.astype(BF16)
        u_lo = (u - u_hi.astype(F32)).astype(BF16)
        bu = (jnp.dot(u_hi, bmat, preferred_element_type=F32)
              + jnp.dot(u_lo, bmat, preferred_element_type=F32))
        re, im = (ar * re - ai * im + bu[:, :SSM_SW], ar * im + ai * re + bu[:, SSM_SW:])
        hcat = jnp.concatenate([re, im], axis=1).astype(BF16)
        y = jnp.dot(hcat, c_ref[0], preferred_element_type=F32) + d_ref[0] * u
        y_ref[t] = jax.nn.gelu(y).astype(BF16)
    sre_ref[...] = re
    sim_ref[...] = im


def _ssm_step(z, tables, h_re, h_im, batch, steps):
    bbd, cbd, d, apow, _ = tables
    u = z[:, V_END:U_END].reshape(batch, steps, SSM_WIDTH).transpose(1, 0, 2)
    flat = lambda s: s.reshape(batch, SSM_GROUPS * SSM_STATE)
    y, sre, sim = pl.pallas_call(
        _ssm_step_kernel,
        grid=(SSM_NB,),
        in_specs=[
            pl.BlockSpec((steps, batch, LANES), lambda g: (0, 0, g)),
            pl.BlockSpec((1, LANES, 2 * SSM_SW), lambda g: (g, 0, 0)),
            pl.BlockSpec((1, 2 * SSM_SW, LANES), lambda g: (g, 0, 0)),
            pl.BlockSpec((1, 1, LANES), lambda g: (g, 0, 0)),
            pl.BlockSpec((1, SUBLANES, 2 * SSM_SW), lambda g: (g, 0, 0)),
            pl.BlockSpec((batch, SSM_SW), lambda g: (0, g)),
            pl.BlockSpec((batch, SSM_SW), lambda g: (0, g)),
        ],
        out_specs=[
            pl.BlockSpec((steps, batch, LANES), lambda g: (0, 0, g)),
            pl.BlockSpec((batch, SSM_SW), lambda g: (0, g)),
            pl.BlockSpec((batch, SSM_SW), lambda g: (0, g)),
        ],
        out_shape=[jax.ShapeDtypeStruct((steps, batch, SSM_WIDTH), BF16),
                   jax.ShapeDtypeStruct((batch, SSM_GROUPS * SSM_STATE), F32),
                   jax.ShapeDtypeStruct((batch, SSM_GROUPS * SSM_STATE), F32)],
        compiler_params=_params(("parallel",)),
        name="ssm_step",
    )(u, bbd, cbd, d, apow, flat(h_re), flat(h_im))
    y = y.transpose(1, 0, 2).reshape(batch * steps, SSM_WIDTH)
    state = lambda s: s.reshape(batch, SSM_GROUPS, SSM_STATE)
    return y, state(sre), state(sim)


def _merge_kernel(att_ref, ys_ref, za_ref, zs_ref, wup_ref, wga_ref, wgb_ref, o_ref):
    ys = ys_ref[...]
    branch_a = jnp.dot(att_ref[...], wup_ref[...], preferred_element_type=F32)
    glu_a = jnp.dot(ys, wga_ref[...], preferred_element_type=F32)
    glu_b = jnp.dot(ys, wgb_ref[...], preferred_element_type=F32)
    branch_s = glu_a * jax.nn.sigmoid(glu_b)
    merged = jax.nn.sigmoid(za_ref[...]) * branch_a + jax.nn.sigmoid(zs_ref[...]) * branch_s
    o_ref[...] = merged.astype(BF16)


def _merge(att, ys, z, w_up, w_glu, tm, tn):
    n = att.shape[0]
    nj = D_MODEL // tn
    act = pl.BlockSpec((tm, ATTN_WIDTH), lambda j, i: (i, 0))
    gate = lambda off: pl.BlockSpec((tm, tn), lambda j, i: (i, off + j))
    wcol = lambda off: pl.BlockSpec((ATTN_WIDTH, tn), lambda j, i: (0, off + j))
    return pl.pallas_call(
        _merge_kernel,
        grid=(nj, n // tm),
        in_specs=[act, act, gate(U_END // tn), gate((U_END + D_MODEL) // tn),
                  wcol(0), wcol(0), wcol(nj)],
        out_specs=pl.BlockSpec((tm, tn), lambda j, i: (i, j)),
        out_shape=jax.ShapeDtypeStruct((n, D_MODEL), BF16),
        compiler_params=_params(("parallel", "parallel")),
        name="merge",
    )(att, ys, z, z, w_up, w_glu, w_glu)


def _out_proj_kernel(x_ref, m_ref, w_ref, g_ref, x1_ref, hn_ref):
    x1 = x_ref[...] + jnp.dot(m_ref[...], w_ref[...], preferred_element_type=F32)
    x1_ref[...] = x1
    ms = jnp.mean(x1 * x1, axis=-1, keepdims=True)
    hn_ref[...] = (x1 * lax.rsqrt(ms + RMS_EPS) * g_ref[...]).astype(BF16)


def _out_proj(x, merged, w_out, g_ffn, tm):
    n = x.shape[0]
    row = pl.BlockSpec((tm, D_MODEL), lambda i: (i, 0))
    return pl.pallas_call(
        _out_proj_kernel,
        grid=(n // tm,),
        in_specs=[row, row, pl.BlockSpec((D_MODEL, D_MODEL), lambda i: (0, 0)),
                  pl.BlockSpec((1, D_MODEL), lambda i: (0, 0))],
        out_specs=[row, row],
        out_shape=[jax.ShapeDtypeStruct((n, D_MODEL), F32),
                   jax.ShapeDtypeStruct((n, D_MODEL), BF16)],
        compiler_params=_params(("parallel",)),
        name="out_proj",
    )(x, merged, w_out, g_ffn)


def _top_rows(s, k):
    n = s.shape[0]
    row = lax.broadcasted_iota(jnp.int32, s.shape, 0)
    vals, idxs = [], []
    for _ in range(k):
        best = jnp.max(s, axis=0, keepdims=True)
        pick = jnp.min(jnp.where(s == best, row, n), axis=0, keepdims=True)
        vals.append(best)
        idxs.append(pick)
        s = jnp.where(row == pick, -jnp.inf, s)
    return jnp.concatenate(vals, axis=0), jnp.concatenate(idxs, axis=0)


def _router_kernel(h_ref, wq_ref, sk_ref, a_ref, b_ref, g_ref):
    tm = h_ref.shape[0]
    k = PEER_TOPK
    q = jnp.dot(h_ref[...], wq_ref[...], preferred_element_type=F32).astype(BF16)
    half = PEER_DKEY // 2

    n_rows = k + 8 * SUBLANES
    r = lax.broadcasted_iota(jnp.int32, (n_rows, tm), 0)
    p_mid = ((r - k) >> 3) + 1
    q_mid = (r - k) & 7
    valid = (r < k) | (r >= k + 7 * SUBLANES) | ((p_mid + 1) * (q_mid + 1) <= k)
    r16 = lax.broadcasted_iota(jnp.int32, (k, tm), 0)

    def cand_rows(first, second):
        parts = [first[0:1] + second[0:k]]
        parts += [first[p:p + 1] + second[0:SUBLANES] for p in range(1, 8)]
        parts.append(first[8:16] + second[0:1])
        return jnp.concatenate(parts, axis=0)

    for h in range(PEER_HEADS):
        tops = []
        for m in range(2):
            c = 2 * h + m
            s = _nt_dot(sk_ref[m], q[:, c * half:(c + 1) * half])
            tops.append(_top_rows(s, k))
        (v1, i1), (v2, i2) = tops
        cand = jnp.where(valid, cand_rows(v1, v2), -jnp.inf)
        fv, frow = _top_rows(cand, k)
        tail = k + 7 * SUBLANES
        p_pos = jnp.where(frow < k, 0, jnp.where(frow >= tail, frow - tail + SUBLANES, ((frow - k) >> 3) + 1))
        q_pos = jnp.where(frow < k, frow, jnp.where(frow >= tail, 0, (frow - k) & 7))
        sel_a, sel_b = [], []
        for j in range(k):
            sel_a.append(jnp.max(jnp.where(r16 == p_pos[j:j + 1], i1, -1), axis=0, keepdims=True))
            sel_b.append(jnp.max(jnp.where(r16 == q_pos[j:j + 1], i2, -1), axis=0, keepdims=True))
        e = jnp.exp(fv - fv[0:1])
        gate = e / jnp.sum(e, axis=0, keepdims=True)
        a_ref[h * k:(h + 1) * k, :] = jnp.concatenate(sel_a, axis=0)
        b_ref[h * k:(h + 1) * k, :] = jnp.concatenate(sel_b, axis=0)
        g_ref[h * k:(h + 1) * k, :] = gate


def _router(hn, w_query, sub_keys, tm):
    n = hn.shape[0]
    slots = PEER_HEADS * PEER_TOPK
    out = pl.BlockSpec((slots, tm), lambda i: (0, i))
    return pl.pallas_call(
        _router_kernel,
        grid=(n // tm,),
        in_specs=[pl.BlockSpec((tm, D_MODEL), lambda i: (i, 0)),
                  pl.BlockSpec((D_MODEL, PEER_HEADS * PEER_DKEY), lambda i: (0, 0)),
                  pl.BlockSpec((2, PEER_KEYS, PEER_DKEY // 2), lambda i: (0, 0, 0))],
        out_specs=[out, out, out],
        out_shape=[jax.ShapeDtypeStruct((slots, n), jnp.int32),
                   jax.ShapeDtypeStruct((slots, n), jnp.int32),
                   jax.ShapeDtypeStruct((slots, n), F32)],
        compiler_params=_params(("parallel",)),
        name="router",
    )(hn, w_query, sub_keys)


def _wbuild_kernel(a_ref, b_ref, g_ref, w_ref):
    tm = a_ref.shape[0]
    idx = lax.broadcasted_iota(jnp.int32, (PEER_KEYS, a_ref.shape[1]), 0)
    sub = lax.broadcasted_iota(jnp.int32, (PEER_KEYS, PEER_KEYS), 0) & (SUBLANES - 1)

    def body(i, carry):
        base = pl.multiple_of(i * WBUILD_UNROLL, WBUILD_UNROLL)
        a_rows = a_ref[pl.ds(base, WBUILD_UNROLL), :]
        b_rows = b_ref[pl.ds(base, WBUILD_UNROLL), :]
        g_rows = g_ref[pl.ds(base, WBUILD_UNROLL), :]
        planes = []
        for t in range(WBUILD_UNROLL):
            first = jnp.where(a_rows[t:t + 1] == idx, g_rows[t:t + 1], 0.0).astype(BF16)
            second = jnp.where(b_rows[t:t + 1] == idx, 1.0, 0.0).astype(BF16)
            planes.append(_nt_dot(first, second))
        for d in (4, 2, 1):
            upper = (sub & d) != 0
            for t in range(WBUILD_UNROLL):
                if t & d == 0:
                    lo, hi = planes[t], planes[t + d]
                    planes[t] = jnp.where(upper, pltpu.roll(hi, d, axis=0), lo)
                    planes[t + d] = jnp.where(upper, hi, pltpu.roll(lo, PEER_KEYS - d, axis=0))
        for r in range(WBUILD_UNROLL):
            w_ref[i, :, r] = planes[r].reshape(PEER_KEYS // SUBLANES, SUBLANES, PEER_KEYS)
        return carry

    lax.fori_loop(0, tm // WBUILD_UNROLL, body, 0)


def _wbuild(a, b, g, tm):
    n, slots = a.shape
    row = pl.BlockSpec((tm, slots), lambda i: (i, 0))
    key_hi = PEER_KEYS // SUBLANES
    w = pl.pallas_call(
        _wbuild_kernel,
        grid=(n // tm,),
        in_specs=[row, row, row],
        out_specs=pl.BlockSpec((tm // WBUILD_UNROLL, key_hi, SUBLANES, WBUILD_UNROLL, PEER_KEYS),
                               lambda i: (i, 0, 0, 0, 0)),
        out_shape=jax.ShapeDtypeStruct((n // WBUILD_UNROLL, key_hi, SUBLANES, WBUILD_UNROLL, PEER_KEYS), F32),
        compiler_params=_params(("parallel",)),
        name="wbuild",
    )(a, b, g)
    return w.reshape(n // WBUILD_UNROLL, PEER_KEYS, WBUILD_UNROLL, PEER_KEYS)


def _experts_kernel(h_ref, ut_ref, v_ref, w_ref, o_ref):
    @pl.when(pl.program_id(1) == 0)
    def _():
        o_ref[...] = jnp.zeros_like(o_ref)

    tm = h_ref.shape[0]
    s = jnp.dot(h_ref[...], ut_ref[...], preferred_element_type=F32)
    w = jnp.concatenate([w_ref[:, r].reshape(tm, PEER_KEYS) for r in range(w_ref.shape[1])], axis=1)
    c = (jax.nn.gelu(s) * w).astype(BF16)
    o_ref[...] += jnp.dot(c, v_ref[...], preferred_element_type=F32)


def _experts(hn, u_t, v, w, tm):
    n = hn.shape[0]
    keys_per_tile = EXPERT_TILE // PEER_KEYS
    return pl.pallas_call(
        _experts_kernel,
        grid=(n // tm, PEER_EXPERTS // EXPERT_TILE),
        in_specs=[pl.BlockSpec((tm, D_MODEL), lambda i, j: (i, 0)),
                  pl.BlockSpec((D_MODEL, EXPERT_TILE), lambda i, j: (0, j)),
                  pl.BlockSpec((EXPERT_TILE, D_MODEL), lambda i, j: (j, 0)),
                  pl.BlockSpec((tm // WBUILD_UNROLL, keys_per_tile, WBUILD_UNROLL, PEER_KEYS),
                               lambda i, j: (i, j, 0, 0))],
        out_specs=pl.BlockSpec((tm, D_MODEL), lambda i, j: (i, 0)),
        out_shape=jax.ShapeDtypeStruct((n, D_MODEL), F32),
        compiler_params=_params(("parallel", "arbitrary")),
        name="experts",
    )(hn, u_t, v, w)


def _ple_kernel(x1_ref, peer_ref, p_ref, wp_ref, wg_ref, g_ref, o_ref):
    x2 = x1_ref[...] + peer_ref[...]
    ms = jnp.mean(x2 * x2, axis=-1, keepdims=True)
    hn = (x2 * lax.rsqrt(ms + RMS_EPS) * g_ref[...]).astype(BF16)
    emb = jnp.dot(p_ref[...].astype(BF16), wp_ref[...], preferred_element_type=F32)
    gate = jax.nn.sigmoid(jnp.dot(hn, wg_ref[...], preferred_element_type=F32))
    o_ref[...] = x2 + emb * gate


def _ple(x1, peer, p, w_ple, w_gate, g_ple, tm):
    n = x1.shape[0]
    row = pl.BlockSpec((tm, D_MODEL), lambda i: (i, 0))
    return pl.pallas_call(
        _ple_kernel,
        grid=(n // tm,),
        in_specs=[row, row, pl.BlockSpec((tm, PLE_DIM), lambda i: (i, 0)),
                  pl.BlockSpec((PLE_DIM, D_MODEL), lambda i: (0, 0)),
                  pl.BlockSpec((D_MODEL, D_MODEL), lambda i: (0, 0)),
                  pl.BlockSpec((1, D_MODEL), lambda i: (0, 0))],
        out_specs=row,
        out_shape=jax.ShapeDtypeStruct((n, D_MODEL), F32),
        compiler_params=_params(("parallel",)),
        name="ple",
    )(x1, peer, p, w_ple, w_gate, g_ple)


def _tile(n, pref):
    return pref if n % pref == 0 else n


def _layer(x, p, pos_rows, attention, ssm, wts):
    n = x.shape[0]
    z = _in_proj(x, wts["g_mix"], wts["w_in"], _tile(n, 1024), 1024)
    cos, sin_lo, sin_hi = _rope_tables(pos_rows)
    q, k_f32, k_bf, v_bf = _qkv_prep(z, wts["g_q"], wts["g_k"], wts["bd"], cos, sin_lo, sin_hi,
                                     _tile(cos.shape[0], 256))
    att = attention(q, k_bf, v_bf)
    ys, s_re, s_im = ssm(z)
    merged = _merge(att, ys, z, wts["w_attn_up"], wts["w_glu"], _tile(n, 512), 512)
    x1, hn = _out_proj(x, merged, wts["w_out"], wts["g_ffn"], _tile(n, 256))
    a_t, b_t, g_t = _router(hn, wts["peer_w_query"], wts["peer_sub_keys"], 128)
    w = _wbuild(a_t.T, b_t.T, g_t.T, _tile(n, 64))
    peer = _experts(hn, wts["peer_u_t"], wts["peer_v"], w, _tile(n, 512))
    y = _ple(x1, peer, p, wts["w_ple"], wts["w_ple_gate"], wts["g_ple"], _tile(n, 256))
    k_new = k_f32.reshape(n, N_HEADS, 2 * HEAD_DK)
    v_new = z[:, 2 * QK_COLS:V_END].reshape(n, N_HEADS, HEAD_DV)
    return y, k_new, v_new, s_re, s_im


def kernel(x_prompt, x_sample, cache_k, cache_v, state_ssm_re, state_ssm_im, page_table, p_prompt, p_sample, g_mix, w_in, g_q, g_k, lambda_q, lambda_k, g_head, w_attn_up, ssm_a_re, ssm_a_im, ssm_log_dt, ssm_b_re, ssm_b_im, ssm_c_re, ssm_c_im, ssm_d, w_glu, w_out, g_ffn, peer_w_query, peer_sub_keys, peer_u, peer_v, g_ple, w_ple, w_ple_gate):
    depth = w_in.shape[0]
    assert depth == 1
    batch, seq, _ = x_prompt.shape
    n_dec, n_new, _ = x_sample.shape
    past_len = page_table.shape[1] * PAGE_SIZE
    i = 0
    lam_init = 0.8 - 0.6 * math.exp(-0.3 * i)

    row = lambda t: t.reshape(1, -1)
    bd = jnp.kron(jnp.eye(2, dtype=F32), jnp.ones((HEAD_DK, HEAD_DK), F32)).astype(BF16)
    wts = {
        "g_mix": row(g_mix[i]), "w_in": w_in[i].astype(BF16),
        "g_q": row(g_q[i]), "g_k": row(g_k[i]), "bd": bd,
        "w_attn_up": w_attn_up[i].astype(BF16), "w_glu": w_glu[i].astype(BF16),
        "w_out": w_out[i].astype(BF16), "g_ffn": row(g_ffn[i]),
        "peer_w_query": peer_w_query[i].astype(BF16), "peer_sub_keys": peer_sub_keys[i].astype(BF16),
        "peer_u_t": peer_u[i].T.astype(BF16), "peer_v": peer_v[i].astype(BF16),
        "g_ple": row(g_ple[i]), "w_ple": w_ple[i].astype(BF16), "w_ple_gate": w_ple_gate[i].astype(BF16),
    }
    g_head2 = row(g_head[i])
    lq, lk = lambda_q[i].astype(F32), lambda_k[i].astype(F32)
    lam = (jnp.exp(jnp.sum(lq[0] * lk[0])) - jnp.exp(jnp.sum(lq[1] * lk[1])) + lam_init).reshape(1)
    ssm_args = (ssm_a_re[i], ssm_a_im[i], ssm_log_dt[i], ssm_b_re[i], ssm_b_im[i],
                ssm_c_re[i], ssm_c_im[i], ssm_d[i])
    tables_prompt = tables_step = _ssm_tables(*ssm_args)

    pos_prompt = jnp.arange(seq, dtype=jnp.int32)
    y_p, k_p, v_p, sre_p, sim_p = _layer(
        x_prompt.reshape(batch * seq, D_MODEL), p_prompt[i].reshape(batch * seq, PLE_DIM), pos_prompt,
        lambda q, k, v: _flash_attention(q, k, v, g_head2, lam, lam_init, batch, seq, FLASH_BLOCK),
        lambda z: _ssm_prompt(z, tables_prompt, batch, seq),
        wts)

    pos_sample = jnp.tile(past_len + jnp.arange(n_new, dtype=jnp.int32), n_dec)
    y_s, k_s, v_s, sre_s, sim_s = _layer(
        x_sample.reshape(n_dec * n_new, D_MODEL), p_sample[i].reshape(n_dec * n_new, PLE_DIM), pos_sample,
        lambda q, k, v: _decode_attention(q, k, v, cache_k[i], cache_v[i], page_table, g_head2, lam,
                                          lam_init, n_dec, n_new),
        lambda z: _ssm_step(z, tables_step, state_ssm_re[i], state_ssm_im[i], n_dec, n_new),
        wts)

    lead = lambda t, *shape: t.reshape(1, *shape)
    return (y_p.reshape(batch, seq, D_MODEL), y_s.reshape(n_dec, n_new, D_MODEL),
            lead(k_p, batch, seq, N_HEADS, 2 * HEAD_DK), lead(v_p, batch, seq, N_HEADS, HEAD_DV),
            lead(sre_p, batch, SSM_GROUPS, SSM_STATE), lead(sim_p, batch, SSM_GROUPS, SSM_STATE),
            lead(k_s, n_dec, n_new, N_HEADS, 2 * HEAD_DK), lead(v_s, n_dec, n_new, N_HEADS, HEAD_DV),
            lead(sre_s, n_dec, SSM_GROUPS, SSM_STATE), lead(sim_s, n_dec, SSM_GROUPS, SSM_STATE))
```

```python
import functools
import math

import jax
import jax.numpy as jnp
from jax import lax
from jax.experimental import pallas as pl
from jax.experimental.pallas import tpu as pltpu

F32 = jnp.float32
BF16 = jnp.bfloat16

D_MODEL = 2048
PAGE_SIZE = 128
N_HEADS = 8
HEAD_DK = 64
HEAD_DV = 2 * HEAD_DK
QK_COLS = N_HEADS * 2 * HEAD_DK
ATTN_WIDTH = N_HEADS * HEAD_DV
ROPE_DIMS = HEAD_DK // 4
ROPE_THETA = 500000.0
NEG_INF = -1e30
SSM_WIDTH = D_MODEL // 2
SSM_GROUP = 16
SSM_GROUPS = SSM_WIDTH // SSM_GROUP
SSM_STATE = 64
PEER_HEADS = 8
PEER_KEYS = 128
PEER_EXPERTS = PEER_KEYS * PEER_KEYS
PEER_DKEY = 256
PEER_TOPK = 16
PLE_DIM = 256
RMS_EPS = 1e-6
V_END = 2 * QK_COLS + ATTN_WIDTH
U_END = V_END + SSM_WIDTH
IN_COLS = U_END + 2 * D_MODEL

LANES = 128
SUBLANES = 8
VMEM_LIMIT = 52 * 1024 * 1024
SSM_GB = LANES // SSM_GROUP
SSM_NB = SSM_GROUPS // SSM_GB
SSM_SW = SSM_GB * SSM_STATE
SSM_CHUNK = 128
SCAN_STEPS = (1, 2, 4)
DECODE_PAGES = 8
FLASH_BLOCK = 512
WBUILD_UNROLL = 8
EXPERT_TILE = 1024
QUERY_SCALE = HEAD_DK ** -0.5 * math.log2(math.e)
SAFE_SCORE_BOUND = 60.0


def _params(sem):
    return pltpu.CompilerParams(dimension_semantics=sem, vmem_limit_bytes=VMEM_LIMIT)


def _nt_dot(a, b):
    return lax.dot_general(a, b, (((1,), (1,)), ((), ())), preferred_element_type=F32)


def _in_proj_kernel(x_ref, g_ref, w_ref, o_ref, h_ref):
    @pl.when(pl.program_id(1) == 0)
    def _():
        x = x_ref[...]
        ms = jnp.mean(x * x, axis=-1, keepdims=True)
        h_ref[...] = (x * lax.rsqrt(ms + RMS_EPS) * g_ref[...]).astype(BF16)

    o_ref[...] = jnp.dot(h_ref[...], w_ref[...], preferred_element_type=F32)


def _in_proj(x, g, w_bf, tm, tn):
    n, d = x.shape
    cols = w_bf.shape[1]
    return pl.pallas_call(
        _in_proj_kernel,
        grid=(n // tm, cols // tn),
        in_specs=[
            pl.BlockSpec((tm, d), lambda i, j: (i, 0)),
            pl.BlockSpec((1, d), lambda i, j: (0, 0)),
            pl.BlockSpec((d, tn), lambda i, j: (0, j)),
        ],
        out_specs=pl.BlockSpec((tm, tn), lambda i, j: (i, j)),
        out_shape=jax.ShapeDtypeStruct((n, cols), F32),
        scratch_shapes=[pltpu.VMEM((tm, d), BF16)],
        compiler_params=_params(("parallel", "arbitrary")),
        name="in_proj",
    )(x, g, w_bf)


def _qkv_prep_kernel(zq_ref, zk_ref, zv_ref, gq_ref, gk_ref, bd_ref, c_ref, s1_ref, s2_ref,
                     q_ref, kf_ref, kb_ref, vb_ref):
    bd = bd_ref[...]
    cos, sin_lo, sin_hi = c_ref[...], s1_ref[...], s2_ref[...]

    def norm_rope(x, g):
        x2 = x * x
        hi = x2.astype(BF16)
        lo = (x2 - hi.astype(F32)).astype(BF16)
        ss = (jnp.dot(hi, bd, preferred_element_type=F32)
              + jnp.dot(lo, bd, preferred_element_type=F32))
        xn = x * lax.rsqrt(ss * (1.0 / HEAD_DK) + RMS_EPS) * g
        fwd = pltpu.roll(xn, LANES - ROPE_DIMS // 2, axis=1)
        bwd = pltpu.roll(xn, ROPE_DIMS // 2, axis=1)
        return xn * cos + fwd * sin_lo + bwd * sin_hi

    for c in range(QK_COLS // LANES):
        sl = slice(c * LANES, (c + 1) * LANES)
        q = norm_rope(zq_ref[:, sl], gq_ref[...])
        q_ref[:, sl] = (q * QUERY_SCALE).astype(BF16)
        k = norm_rope(zk_ref[:, sl], gk_ref[...])
        kf_ref[:, sl] = k
        kb_ref[:, sl] = k.astype(BF16)
    vb_ref[...] = zv_ref[...].astype(BF16)


def _qkv_prep(z, gq, gk, bd, cos, sin_lo, sin_hi, tm):
    n = z.shape[0]
    nt = cos.shape[0] // tm
    row = lambda c: pl.BlockSpec((tm, QK_COLS), lambda i: (i, c))
    vec = pl.BlockSpec((1, LANES), lambda i: (0, 0))
    tab = pl.BlockSpec((tm, LANES), lambda i: (i % nt, 0))
    out = pl.BlockSpec((tm, QK_COLS), lambda i: (i, 0))
    return pl.pallas_call(
        _qkv_prep_kernel,
        grid=(n // tm,),
        in_specs=[row(0), row(1), row(2), vec, vec,
                  pl.BlockSpec((LANES, LANES), lambda i: (0, 0)), tab, tab, tab],
        out_specs=[out, out, out, out],
        out_shape=[jax.ShapeDtypeStruct((n, QK_COLS), BF16),
                   jax.ShapeDtypeStruct((n, QK_COLS), F32),
                   jax.ShapeDtypeStruct((n, QK_COLS), BF16),
                   jax.ShapeDtypeStruct((n, ATTN_WIDTH), BF16)],
        compiler_params=_params(("parallel",)),
        name="qkv_prep",
    )(z, z, z, gq, gk, bd, cos, sin_lo, sin_hi)


def _rope_tables(pos):
    half = ROPE_DIMS // 2
    inv_freq = ROPE_THETA ** (-jnp.arange(half, dtype=F32) / half)
    ang = pos.astype(F32)[:, None] * inv_freq[None, :]
    cos, sin = jnp.cos(ang), jnp.sin(ang)
    ones = jnp.ones((pos.shape[0], HEAD_DK - ROPE_DIMS), F32)
    zeros8 = jnp.zeros_like(sin)
    zeros = jnp.zeros_like(ones)
    c = jnp.concatenate([cos, cos, ones], axis=1)
    s_lo = jnp.concatenate([-sin, zeros8, zeros], axis=1)
    s_hi = jnp.concatenate([zeros8, sin, zeros], axis=1)
    two = lambda a: jnp.concatenate([a, a], axis=1)
    return two(c), two(s_lo), two(s_hi)


def _head_finish(o, g, lam_init):
    ms = jnp.mean(o * o, axis=-1, keepdims=True)
    return o * lax.rsqrt(ms + RMS_EPS) * g * (1.0 - lam_init)


def _flash_kernel(lam_ref, q_ref, k_ref, v_ref, g_ref, o_ref, *, blk, lam_init):
    qi = pl.program_id(2)
    q = q_ref[0]
    lane = lax.broadcasted_iota(jnp.int32, q.shape, 1)
    zero = jnp.zeros_like(q)
    qq = jnp.concatenate([jnp.where(lane < HEAD_DK, q, zero),
                          jnp.where(lane >= HEAD_DK, q, zero)], axis=0)

    lam, bound = lam_ref[0], lam_ref[1]

    def scores(ki, masked):
        start = pl.multiple_of(ki * blk, blk)
        s = _nt_dot(qq, k_ref[0, pl.ds(start, blk), :])
        if masked:
            r = lax.broadcasted_iota(jnp.int32, s.shape, 0)
            c = lax.broadcasted_iota(jnp.int32, s.shape, 1)
            s = jnp.where(jnp.where(r >= blk, r - blk, r) >= c, s, NEG_INF)
        return s, v_ref[0, pl.ds(start, blk), :]

    def finish(l, acc):
        o = acc / l
        o = o[:blk] - lam * o[blk:]
        o_ref[0] = _head_finish(o, g_ref[...], lam_init).astype(BF16)

    zeros = (jnp.zeros((2 * blk, 1), F32), jnp.zeros((2 * blk, HEAD_DV), F32))

    @pl.when(bound <= SAFE_SCORE_BOUND)
    def _():
        def step(ki, carry, masked):
            l, acc = carry
            s, v = scores(ki, masked)
            p = jnp.exp2(s - bound)
            return (l + jnp.sum(p, axis=1, keepdims=True),
                    acc + jnp.dot(p.astype(BF16), v, preferred_element_type=F32))

        carry = lax.fori_loop(0, qi, lambda ki, c: step(ki, c, False), zeros)
        finish(*step(qi, carry, True))

    @pl.when(bound > SAFE_SCORE_BOUND)
    def _():
        def step(ki, carry, masked):
            m, l, acc = carry
            s, v = scores(ki, masked)
            m_new = jnp.maximum(m, jnp.max(s, axis=1, keepdims=True))
            alpha = jnp.exp2(m - m_new)
            p = jnp.exp2(s - m_new)
            return (m_new, alpha * l + jnp.sum(p, axis=1, keepdims=True),
                    alpha * acc + jnp.dot(p.astype(BF16), v, preferred_element_type=F32))

        init = (jnp.full((2 * blk, 1), NEG_INF, F32),) + zeros
        carry = lax.fori_loop(0, qi, lambda ki, c: step(ki, c, False), init)
        finish(*step(qi, carry, True)[1:])


def _flash_attention(q, k, v, g_head, lam, lam_init, batch, seq, blk):
    q3, k3, v3 = (t.reshape(batch, seq, QK_COLS) for t in (q, k, v))
    whole = pl.BlockSpec((1, seq, LANES), lambda b, h, i: (b, 0, h))
    out = pl.pallas_call(
        functools.partial(_flash_kernel, blk=blk, lam_init=lam_init),
        grid=(batch, N_HEADS, seq // blk),
        in_specs=[
            pl.BlockSpec(memory_space=pltpu.SMEM),
            pl.BlockSpec((1, blk, LANES), lambda b, h, i: (b, i, h)),
            whole, whole,
            pl.BlockSpec((1, LANES), lambda b, h, i: (0, 0)),
        ],
        out_specs=pl.BlockSpec((1, blk, LANES), lambda b, h, i: (b, i, h)),
        out_shape=jax.ShapeDtypeStruct((batch, seq, ATTN_WIDTH), BF16),
        compiler_params=_params(("parallel", "parallel", "arbitrary")),
        name="flash_attention",
    )(lam, q3, k3, v3, g_head)
    return out.reshape(batch * seq, ATTN_WIDTH)


def _decode_kernel(pt_ref, lam_ref, qt_ref, bias_ref, biasn_ref, kn_ref, vn_ref, g_ref, *rest,
                   n_rows, lam_init):
    k_refs, v_refs = rest[:DECODE_PAGES], rest[DECODE_PAGES:2 * DECODE_PAGES]
    o_ref, m_ref, l_ref, acc_ref, s_ref = rest[2 * DECODE_PAGES:]
    step = pl.program_id(1)
    is_last = step == pl.num_programs(1) - 1
    page_rows = PAGE_SIZE * N_HEADS

    @pl.when(step == 0)
    def _():
        m_ref[...] = jnp.full_like(m_ref, NEG_INF)
        l_ref[...] = jnp.zeros_like(l_ref)
        acc_ref[...] = jnp.zeros_like(acc_ref)

    eye = (lax.broadcasted_iota(jnp.int32, (LANES, LANES), 0)
           == lax.broadcasted_iota(jnp.int32, (LANES, LANES), 1))

    def to_col(row):
        return jnp.sum(jnp.where(eye, jnp.broadcast_to(row, (LANES, LANES)), 0.0), axis=1, keepdims=True)

    def contract_rows(p, v):
        return lax.dot_general(p.astype(BF16), v, (((0,), (0,)), ((), ())), preferred_element_type=F32)

    qt = qt_ref[0]
    bias = bias_ref[...]
    m_prev = m_ref[...]
    m_new = m_prev
    for r in range(DECODE_PAGES):
        k2d = k_refs[r][...].reshape(page_rows, LANES).astype(BF16)
        s = jnp.dot(k2d, qt, preferred_element_type=F32) + bias
        s_ref[r] = s
        m_new = jnp.maximum(m_new, jnp.max(s, axis=0, keepdims=True))
    s_new = (jnp.dot(kn_ref[0], qt, preferred_element_type=F32) + biasn_ref[...]
             + jnp.where(is_last, 0.0, NEG_INF))
    m_new = jnp.maximum(m_new, jnp.max(s_new, axis=0, keepdims=True))

    alpha = jnp.exp2(m_prev - m_new)
    p_new = jnp.exp2(s_new - m_new)
    l = alpha * l_ref[...] + jnp.sum(p_new, axis=0, keepdims=True)
    pv = contract_rows(p_new, vn_ref[0])
    for r in range(DECODE_PAGES):
        p = jnp.exp2(s_ref[r] - m_new)
        l = l + jnp.sum(p, axis=0, keepdims=True)
        pv = pv + contract_rows(p, v_refs[r][...].reshape(page_rows, LANES).astype(BF16))
    acc = to_col(alpha) * acc_ref[...] + pv
    acc_ref[...] = acc
    m_ref[...] = m_new
    l_ref[...] = l

    @pl.when(is_last)
    def _():
        o = acc / to_col(l)
        o = o[:n_rows] - lam_ref[0] * o[n_rows:2 * n_rows]
        o_ref[0] = _head_finish(o, g_ref[...], lam_init).astype(BF16)


def _decode_attention(q, k_new, v_new, cache_k, cache_v, page_table, g_head, lam, lam_init, n_dec, n_new):
    n_pages = page_table.shape[1]
    n_rows = N_HEADS * n_new
    q5 = q.reshape(n_dec, n_new, N_HEADS, 2, HEAD_DK)
    qt = jnp.einsum("bthmd,mM->bMdmht", q5, jnp.eye(2, dtype=BF16)).reshape(n_dec, LANES, 2 * n_rows)
    qt = jnp.pad(qt, ((0, 0), (0, 0), (0, LANES - 2 * n_rows)))
    col = jnp.arange(LANES)
    col_ok, col_head, col_t = col < 2 * n_rows, (col % n_rows) // n_new, col % n_new
    row = jnp.arange(PAGE_SIZE * N_HEADS)
    bias = jnp.where(col_ok[None] & (row[:, None] % N_HEADS == col_head[None]), 0.0, NEG_INF).astype(F32)
    row_n = jnp.arange(n_rows)
    bias_new = jnp.where(col_ok[None] & (row_n[:, None] % N_HEADS == col_head[None])
                         & (row_n[:, None] // N_HEADS <= col_t[None]), 0.0, NEG_INF).astype(F32)
    kn = k_new.reshape(n_dec, n_rows, LANES)
    vn = v_new.reshape(n_dec, n_rows, LANES)

    def page_spec(r):
        return pl.BlockSpec((None, PAGE_SIZE, N_HEADS, LANES),
                            lambda b, s, pt: (pt[b * n_pages + s * DECODE_PAGES + r], 0, 0, 0))

    const = lambda shape: pl.BlockSpec(shape, lambda b, s, pt: (0,) * len(shape))
    per_b = lambda shape: pl.BlockSpec(shape, lambda b, s, pt: (b,) + (0,) * (len(shape) - 1))
    pages = [page_spec(r) for r in range(DECODE_PAGES)]
    grid_spec = pltpu.PrefetchScalarGridSpec(
        num_scalar_prefetch=1,
        grid=(n_dec, n_pages // DECODE_PAGES),
        in_specs=[pl.BlockSpec(memory_space=pltpu.SMEM), per_b((1, LANES, LANES)),
                  const((PAGE_SIZE * N_HEADS, LANES)), const((n_rows, LANES)),
                  per_b((1, n_rows, LANES)), per_b((1, n_rows, LANES)), const((1, LANES))]
                 + pages + pages,
        out_specs=per_b((1, n_rows, HEAD_DV)),
        scratch_shapes=[pltpu.VMEM((1, LANES), F32), pltpu.VMEM((1, LANES), F32),
                        pltpu.VMEM((LANES, HEAD_DV), F32),
                        pltpu.VMEM((DECODE_PAGES, PAGE_SIZE * N_HEADS, LANES), F32)],
    )
    out = pl.pallas_call(
        functools.partial(_decode_kernel, n_rows=n_rows, lam_init=lam_init),
        grid_spec=grid_spec,
        out_shape=jax.ShapeDtypeStruct((n_dec, n_rows, HEAD_DV), BF16),
        compiler_params=_params(("parallel", "arbitrary")),
        name="decode_attention",
    )(page_table.reshape(-1), lam, qt, bias, bias_new, kn, vn, g_head,
      *([cache_k] * DECODE_PAGES), *([cache_v] * DECODE_PAGES))
    out = out.reshape(n_dec, N_HEADS, n_new, HEAD_DV).transpose(0, 2, 1, 3)
    return out.reshape(n_dec * n_new, ATTN_WIDTH)


def _ssm_tables(a_re, a_im, log_dt, b_re, b_im, c_re, c_im, d_skip):
    dt = jnp.exp(log_dt)[:, None]
    mag = jnp.exp(a_re * dt)
    ab_re, ab_im = mag * jnp.cos(a_im * dt), mag * jnp.sin(a_im * dt)
    den = a_re * a_re + a_im * a_im
    f_re = ((ab_re - 1.0) * a_re + ab_im * a_im) / den
    f_im = (ab_im * a_re - (ab_re - 1.0) * a_im) / den
    bb_re = f_re[..., None] * b_re - f_im[..., None] * b_im
    bb_im = f_re[..., None] * b_im + f_im[..., None] * b_re
    eye = jnp.eye(SSM_GB, dtype=F32)

    def b_blocks(bb):
        bb = bb.reshape(SSM_NB, SSM_GB, SSM_STATE, SSM_GROUP)
        return jnp.einsum("agpc,gh->agchp", bb, eye).reshape(SSM_NB, LANES, SSM_SW)

    def c_blocks(cc):
        cc = cc.reshape(SSM_NB, SSM_GB, SSM_GROUP, SSM_STATE)
        return jnp.einsum("agcp,gh->agphc", cc, eye).reshape(SSM_NB, SSM_SW, LANES)

    bbd = jnp.concatenate([b_blocks(bb_re), b_blocks(bb_im)], axis=2).astype(BF16)
    cbd = jnp.concatenate([c_blocks(c_re), c_blocks(-c_im)], axis=1).astype(BF16)
    d = d_skip.reshape(SSM_NB, 1, LANES)
    pr, pi = ab_re[None], ab_im[None]
    while pr.shape[0] < SUBLANES:
        tr, ti = pr[-1:], pi[-1:]
        pr, pi = (jnp.concatenate([pr, pr * tr - pi * ti], axis=0),
                  jnp.concatenate([pi, pr * ti + pi * tr], axis=0))
    lay = lambda t: t.reshape(t.shape[0], SSM_NB, SSM_SW).transpose(1, 0, 2)
    apow = jnp.concatenate([lay(pr), lay(pi)], axis=2)
    rows = jnp.arange(SUBLANES)[None, :, None]
    steps = [jnp.where(rows >= dd, apow[:, dd - 1:dd, :], 0.0) for dd in SCAN_STEPS]
    amask = jnp.stack(steps, axis=1)
    return bbd, cbd, d, apow, amask


def _ssm_prompt_kernel(u_ref, b_ref, c_ref, d_ref, ap_ref, am_ref, y_ref, sre_ref, sim_ref, h_ref):
    chunk = pl.program_id(2)

    @pl.when(chunk == 0)
    def _():
        h_ref[...] = jnp.zeros_like(h_ref)

    u = u_ref[0]
    length = u.shape[0]
    bu = jnp.dot(u.astype(BF16), b_ref[0], preferred_element_type=F32)
    pr, pi = ap_ref[0, :, :SSM_SW], ap_ref[0, :, SSM_SW:]
    cr, ci = h_ref[0:1, :SSM_SW], h_ref[0:1, SSM_SW:]
    out_re, out_im = [], []
    for j in range(length // SUBLANES):
        rows = slice(j * SUBLANES, (j + 1) * SUBLANES)
        re, im = bu[rows, :SSM_SW], bu[rows, SSM_SW:]
        for si, dd in enumerate(SCAN_STEPS):
            mr, mi = am_ref[0, si, :, :SSM_SW], am_ref[0, si, :, SSM_SW:]
            sr, s_i = pltpu.roll(re, dd, axis=0), pltpu.roll(im, dd, axis=0)
            re, im = re + mr * sr - mi * s_i, im + mr * s_i + mi * sr
        re, im = re + pr * cr - pi * ci, im + pr * ci + pi * cr
        cr, ci = re[SUBLANES - 1:SUBLANES], im[SUBLANES - 1:SUBLANES]
        out_re.append(re)
        out_im.append(im)
    h_ref[0:1, :SSM_SW] = cr
    h_ref[0:1, SSM_SW:] = ci
    hcat = jnp.concatenate([jnp.concatenate(out_re, axis=0), jnp.concatenate(out_im, axis=0)],
                           axis=1).astype(BF16)
    y = jnp.dot(hcat, c_ref[0], preferred_element_type=F32) + d_ref[0] * u
    y_ref[0] = jax.nn.gelu(y).astype(BF16)

    @pl.when(chunk == pl.num_programs(2) - 1)
    def _():
        sre_ref[0, 0] = cr
        sim_ref[0, 0] = ci


def _ssm_prompt(z, tables, batch, seq):
    bbd, cbd, d, apow, amask = tables
    length = SSM_CHUNK
    z3 = z.reshape(batch, seq, IN_COLS)
    u_blk = V_END // LANES
    y, sre, sim = pl.pallas_call(
        _ssm_prompt_kernel,
        grid=(batch, SSM_NB, seq // length),
        in_specs=[
            pl.BlockSpec((1, length, LANES), lambda b, g, c: (b, c, u_blk + g)),
            pl.BlockSpec((1, LANES, 2 * SSM_SW), lambda b, g, c: (g, 0, 0)),
            pl.BlockSpec((1, 2 * SSM_SW, LANES), lambda b, g, c: (g, 0, 0)),
            pl.BlockSpec((1, 1, LANES), lambda b, g, c: (g, 0, 0)),
            pl.BlockSpec((1, SUBLANES, 2 * SSM_SW), lambda b, g, c: (g, 0, 0)),
            pl.BlockSpec((1, len(SCAN_STEPS), SUBLANES, 2 * SSM_SW), lambda b, g, c: (g, 0, 0, 0)),
        ],
        out_specs=[
            pl.BlockSpec((1, length, LANES), lambda b, g, c: (b, c, g)),
            pl.BlockSpec((1, 1, 1, SSM_SW), lambda b, g, c: (b, g, 0, 0)),
            pl.BlockSpec((1, 1, 1, SSM_SW), lambda b, g, c: (b, g, 0, 0)),
        ],
        out_shape=[jax.ShapeDtypeStruct((batch, seq, SSM_WIDTH), BF16),
                   jax.ShapeDtypeStruct((batch, SSM_NB, 1, SSM_SW), F32),
                   jax.ShapeDtypeStruct((batch, SSM_NB, 1, SSM_SW), F32)],
        scratch_shapes=[pltpu.VMEM((SUBLANES, 2 * SSM_SW), F32)],
        compiler_params=_params(("parallel", "parallel", "arbitrary")),
        name="ssm_prompt",
    )(z3, bbd, cbd, d, apow, amask)
    state = lambda s: s.reshape(batch, SSM_GROUPS, SSM_STATE)
    return y.reshape(batch * seq, SSM_WIDTH), state(sre), state(sim)


def _ssm_step_kernel(u_ref, b_ref, c_ref, d_ref, ap_ref, hre_ref, him_ref, y_ref, sre_ref, sim_ref):
    ar, ai = ap_ref[0, 0:1, :SSM_SW], ap_ref[0, 0:1, SSM_SW:]
    re, im = hre_ref[...], him_ref[...]
    bmat = b_ref[0]
    for t in range(u_ref.shape[0]):
        u = u_ref[t]
        u_hi = u.astype(BF16)
        u_lo = (u - u_hi.astype(F32)).astype(BF16)
        bu = (jnp.dot(u_hi, bmat, preferred_element_type=F32)
              + jnp.dot(u_lo, bmat, preferred_element_type=F32))
        re, im = (ar * re - ai * im + bu[:, :SSM_SW], ar * im + ai * re + bu[:, SSM_SW:])
        hcat = jnp.concatenate([re, im], axis=1).astype(BF16)
        y = jnp.dot(hcat, c_ref[0], preferred_element_type=F32) + d_ref[0] * u
        y_ref[t] = jax.nn.gelu(y).astype(BF16)
    sre_ref[...] = re
    sim_ref[...] = im


def _ssm_step(z, tables, h_re, h_im, batch, steps):
    bbd, cbd, d, apow, _ = tables
    u = z[:, V_END:U_END].reshape(batch, steps, SSM_WIDTH).transpose(1, 0, 2)
    flat = lambda s: s.reshape(batch, SSM_GROUPS * SSM_STATE)
    y, sre, sim = pl.pallas_call(
        _ssm_step_kernel,
        grid=(SSM_NB,),
        in_specs=[
            pl.BlockSpec((steps, batch, LANES), lambda g: (0, 0, g)),
            pl.BlockSpec((1, LANES, 2 * SSM_SW), lambda g: (g, 0, 0)),
            pl.BlockSpec((1, 2 * SSM_SW, LANES), lambda g: (g, 0, 0)),
            pl.BlockSpec((1, 1, LANES), lambda g: (g, 0, 0)),
            pl.BlockSpec((1, SUBLANES, 2 * SSM_SW), lambda g: (g, 0, 0)),
            pl.BlockSpec((batch, SSM_SW), lambda g: (0, g)),
            pl.BlockSpec((batch, SSM_SW), lambda g: (0, g)),
        ],
        out_specs=[
            pl.BlockSpec((steps, batch, LANES), lambda g: (0, 0, g)),
            pl.BlockSpec((batch, SSM_SW), lambda g: (0, g)),
            pl.BlockSpec((batch, SSM_SW), lambda g: (0, g)),
        ],
        out_shape=[jax.ShapeDtypeStruct((steps, batch, SSM_WIDTH), BF16),
                   jax.ShapeDtypeStruct((batch, SSM_GROUPS * SSM_STATE), F32),
                   jax.ShapeDtypeStruct((batch, SSM_GROUPS * SSM_STATE), F32)],
        compiler_params=_params(("parallel",)),
        name="ssm_step",
    )(u, bbd, cbd, d, apow, flat(h_re), flat(h_im))
    y = y.transpose(1, 0, 2).reshape(batch * steps, SSM_WIDTH)
    state = lambda s: s.reshape(batch, SSM_GROUPS, SSM_STATE)
    return y, state(sre), state(sim)


def _merge_kernel(att_ref, ys_ref, za_ref, zs_ref, wup_ref, wga_ref, wgb_ref, o_ref):
    ys = ys_ref[...]
    branch_a = jnp.dot(att_ref[...], wup_ref[...], preferred_element_type=F32)
    glu_a = jnp.dot(ys, wga_ref[...], preferred_element_type=F32)
    glu_b = jnp.dot(ys, wgb_ref[...], preferred_element_type=F32)
    branch_s = glu_a * jax.nn.sigmoid(glu_b)
    merged = jax.nn.sigmoid(za_ref[...]) * branch_a + jax.nn.sigmoid(zs_ref[...]) * branch_s
    o_ref[...] = merged.astype(BF16)


def _merge(att, ys, z, w_up, w_glu, tm, tn):
    n = att.shape[0]
    nj = D_MODEL // tn
    act = pl.BlockSpec((tm, ATTN_WIDTH), lambda j, i: (i, 0))
    gate = lambda off: pl.BlockSpec((tm, tn), lambda j, i: (i, off + j))
    wcol = lambda off: pl.BlockSpec((ATTN_WIDTH, tn), lambda j, i: (0, off + j))
    return pl.pallas_call(
        _merge_kernel,
        grid=(nj, n // tm),
        in_specs=[act, act, gate(U_END // tn), gate((U_END + D_MODEL) // tn),
                  wcol(0), wcol(0), wcol(nj)],
        out_specs=pl.BlockSpec((tm, tn), lambda j, i: (i, j)),
        out_shape=jax.ShapeDtypeStruct((n, D_MODEL), BF16),
        compiler_params=_params(("parallel", "parallel")),
        name="merge",
    )(att, ys, z, z, w_up, w_glu, w_glu)


def _out_proj_kernel(x_ref, m_ref, w_ref, g_ref, x1_ref, hn_ref):
    x1 = x_ref[...] + jnp.dot(m_ref[...], w_ref[...], preferred_element_type=F32)
    x1_ref[...] = x1
    ms = jnp.mean(x1 * x1, axis=-1, keepdims=True)
    hn_ref[...] = (x1 * lax.rsqrt(ms + RMS_EPS) * g_ref[...]).astype(BF16)


def _out_proj(x, merged, w_out, g_ffn, tm):
    n = x.shape[0]
    row = pl.BlockSpec((tm, D_MODEL), lambda i: (i, 0))
    return pl.pallas_call(
        _out_proj_kernel,
        grid=(n // tm,),
        in_specs=[row, row, pl.BlockSpec((D_MODEL, D_MODEL), lambda i: (0, 0)),
                  pl.BlockSpec((1, D_MODEL), lambda i: (0, 0))],
        out_specs=[row, row],
        out_shape=[jax.ShapeDtypeStruct((n, D_MODEL), F32),
                   jax.ShapeDtypeStruct((n, D_MODEL), BF16)],
        compiler_params=_params(("parallel",)),
        name="out_proj",
    )(x, merged, w_out, g_ffn)


def _top_rows(s, k):
    n = s.shape[0]
    row = lax.broadcasted_iota(jnp.int32, s.shape, 0)
    vals, idxs = [], []
    for _ in range(k):
        best = jnp.max(s, axis=0, keepdims=True)
        pick = jnp.min(jnp.where(s == best, row, n), axis=0, keepdims=True)
        vals.append(best)
        idxs.append(pick)
        s = jnp.where(row == pick, -jnp.inf, s)
    return jnp.concatenate(vals, axis=0), jnp.concatenate(idxs, axis=0)


def _router_kernel(h_ref, wq_ref, sk_ref, a_ref, b_ref, g_ref):
    tm = h_ref.shape[0]
    k = PEER_TOPK
    q = jnp.dot(h_ref[...], wq_ref[...], preferred_element_type=F32).astype(BF16)
    half = PEER_DKEY // 2

    n_rows = k + 8 * SUBLANES
    r = lax.broadcasted_iota(jnp.int32, (n_rows, tm), 0)
    p_mid = ((r - k) >> 3) + 1
    q_mid = (r - k) & 7
    valid = (r < k) | (r >= k + 7 * SUBLANES) | ((p_mid + 1) * (q_mid + 1) <= k)
    r16 = lax.broadcasted_iota(jnp.int32, (k, tm), 0)

    def cand_rows(first, second):
        parts = [first[0:1] + second[0:k]]
        parts += [first[p:p + 1] + second[0:SUBLANES] for p in range(1, 8)]
        parts.append(first[8:16] + second[0:1])
        return jnp.concatenate(parts, axis=0)

    for h in range(PEER_HEADS):
        tops = []
        for m in range(2):
            c = 2 * h + m
            s = _nt_dot(sk_ref[m], q[:, c * half:(c + 1) * half])
            tops.append(_top_rows(s, k))
        (v1, i1), (v2, i2) = tops
        cand = jnp.where(valid, cand_rows(v1, v2), -jnp.inf)
        fv, frow = _top_rows(cand, k)
        tail = k + 7 * SUBLANES
        p_pos = jnp.where(frow < k, 0, jnp.where(frow >= tail, frow - tail + SUBLANES, ((frow - k) >> 3) + 1))
        q_pos = jnp.where(frow < k, frow, jnp.where(frow >= tail, 0, (frow - k) & 7))
        sel_a, sel_b = [], []
        for j in range(k):
            sel_a.append(jnp.max(jnp.where(r16 == p_pos[j:j + 1], i1, -1), axis=0, keepdims=True))
            sel_b.append(jnp.max(jnp.where(r16 == q_pos[j:j + 1], i2, -1), axis=0, keepdims=True))
        e = jnp.exp(fv - fv[0:1])
        gate = e / jnp.sum(e, axis=0, keepdims=True)
        a_ref[h * k:(h + 1) * k, :] = jnp.concatenate(sel_a, axis=0)
        b_ref[h * k:(h + 1) * k, :] = jnp.concatenate(sel_b, axis=0)
        g_ref[h * k:(h + 1) * k, :] = gate


def _router(hn, w_query, sub_keys, tm):
    n = hn.shape[0]
    slots = PEER_HEADS * PEER_TOPK
    out = pl.BlockSpec((slots, tm), lambda i: (0, i))
    return pl.pallas_call(
        _router_kernel,
        grid=(n // tm,),
        in_specs=[pl.BlockSpec((tm, D_MODEL), lambda i: (i, 0)),
                  pl.BlockSpec((D_MODEL, PEER_HEADS * PEER_DKEY), lambda i: (0, 0)),
                  pl.BlockSpec((2, PEER_KEYS, PEER_DKEY // 2), lambda i: (0, 0, 0))],
        out_specs=[out, out, out],
        out_shape=[jax.ShapeDtypeStruct((slots, n), jnp.int32),
                   jax.ShapeDtypeStruct((slots, n), jnp.int32),
                   jax.ShapeDtypeStruct((slots, n), F32)],
        compiler_params=_params(("parallel",)),
        name="router",
    )(hn, w_query, sub_keys)


def _wbuild_kernel(a_ref, b_ref, g_ref, w_ref):
    tm = a_ref.shape[0]
    idx = lax.broadcasted_iota(jnp.int32, (PEER_KEYS, a_ref.shape[1]), 0)
    grouped = (PEER_KEYS // SUBLANES, SUBLANES, PEER_KEYS)
    sub = lax.broadcasted_iota(jnp.int32, grouped, 1)

    def body(i, carry):
        base = pl.multiple_of(i * WBUILD_UNROLL, WBUILD_UNROLL)
        a_rows = a_ref[pl.ds(base, WBUILD_UNROLL), :]
        b_rows = b_ref[pl.ds(base, WBUILD_UNROLL), :]
        g_rows = g_ref[pl.ds(base, WBUILD_UNROLL), :]
        planes = []
        for t in range(WBUILD_UNROLL):
            first = jnp.where(a_rows[t:t + 1] == idx, g_rows[t:t + 1], 0.0).astype(BF16)
            second = jnp.where(b_rows[t:t + 1] == idx, 1.0, 0.0).astype(BF16)
            planes.append(_nt_dot(first, second).reshape(grouped))
        for d in (4, 2, 1):
            upper = (sub & d) != 0
            for t in range(WBUILD_UNROLL):
                if t & d == 0:
                    lo, hi = planes[t], planes[t + d]
                    planes[t] = jnp.where(upper, pltpu.roll(hi, d, axis=1), lo)
                    planes[t + d] = jnp.where(upper, hi, pltpu.roll(lo, SUBLANES - d, axis=1))
        for r in range(WBUILD_UNROLL):
            w_ref[i, :, r] = planes[r]
        return carry

    lax.fori_loop(0, tm // WBUILD_UNROLL, body, 0)


def _wbuild(a, b, g, tm):
    n, slots = a.shape
    row = pl.BlockSpec((tm, slots), lambda i: (i, 0))
    key_hi = PEER_KEYS // SUBLANES
    w = pl.pallas_call(
        _wbuild_kernel,
        grid=(n // tm,),
        in_specs=[row, row, row],
        out_specs=pl.BlockSpec((tm // WBUILD_UNROLL, key_hi, SUBLANES, WBUILD_UNROLL, PEER_KEYS),
                               lambda i: (i, 0, 0, 0, 0)),
        out_shape=jax.ShapeDtypeStruct((n // WBUILD_UNROLL, key_hi, SUBLANES, WBUILD_UNROLL, PEER_KEYS), F32),
        compiler_params=_params(("parallel",)),
        name="wbuild",
    )(a, b, g)
    return w.reshape(n // WBUILD_UNROLL, PEER_KEYS, WBUILD_UNROLL, PEER_KEYS)


def _experts_kernel(h_ref, ut_ref, v_ref, w_ref, o_ref):
    @pl.when(pl.program_id(1) == 0)
    def _():
        o_ref[...] = jnp.zeros_like(o_ref)

    tm = h_ref.shape[0]
    s = jnp.dot(h_ref[...], ut_ref[...], preferred_element_type=F32)
    w = jnp.concatenate([w_ref[:, r].reshape(tm, PEER_KEYS) for r in range(w_ref.shape[1])], axis=1)
    c = (jax.nn.gelu(s) * w).astype(BF16)
    o_ref[...] += jnp.dot(c, v_ref[...], preferred_element_type=F32)


def _experts(hn, u_t, v, w, tm):
    n = hn.shape[0]
    keys_per_tile = EXPERT_TILE // PEER_KEYS
    return pl.pallas_call(
        _experts_kernel,
        grid=(n // tm, PEER_EXPERTS // EXPERT_TILE),
        in_specs=[pl.BlockSpec((tm, D_MODEL), lambda i, j: (i, 0)),
                  pl.BlockSpec((D_MODEL, EXPERT_TILE), lambda i, j: (0, j)),
                  pl.BlockSpec((EXPERT_TILE, D_MODEL), lambda i, j: (j, 0)),
                  pl.BlockSpec((tm // WBUILD_UNROLL, keys_per_tile, WBUILD_UNROLL, PEER_KEYS),
                               lambda i, j: (i, j, 0, 0))],
        out_specs=pl.BlockSpec((tm, D_MODEL), lambda i, j: (i, 0)),
        out_shape=jax.ShapeDtypeStruct((n, D_MODEL), F32),
        compiler_params=_params(("parallel", "arbitrary")),
        name="experts",
    )(hn, u_t, v, w)


def _ple_kernel(x1_ref, peer_ref, p_ref, wp_ref, wg_ref, g_ref, o_ref):
    x2 = x1_ref[...] + peer_ref[...]
    ms = jnp.mean(x2 * x2, axis=-1, keepdims=True)
    hn = (x2 * lax.rsqrt(ms + RMS_EPS) * g_ref[...]).astype(BF16)
    emb = jnp.dot(p_ref[...].astype(BF16), wp_ref[...], preferred_element_type=F32)
    gate = jax.nn.sigmoid(jnp.dot(hn, wg_ref[...], preferred_element_type=F32))
    o_ref[...] = x2 + emb * gate


def _ple(x1, peer, p, w_ple, w_gate, g_ple, tm):
    n = x1.shape[0]
    row = pl.BlockSpec((tm, D_MODEL), lambda i: (i, 0))
    return pl.pallas_call(
        _ple_kernel,
        grid=(n // tm,),
        in_specs=[row, row, pl.BlockSpec((tm, PLE_DIM), lambda i: (i, 0)),
                  pl.BlockSpec((PLE_DIM, D_MODEL), lambda i: (0, 0)),
                  pl.BlockSpec((D_MODEL, D_MODEL), lambda i: (0, 0)),
                  pl.BlockSpec((1, D_MODEL), lambda i: (0, 0))],
        out_specs=row,
        out_shape=jax.ShapeDtypeStruct((n, D_MODEL), F32),
        compiler_params=_params(("parallel",)),
        name="ple",
    )(x1, peer, p, w_ple, w_gate, g_ple)


def _tile(n, pref):
    return pref if n % pref == 0 else n


def _layer(x, p, pos_rows, attention, ssm, wts):
    n = x.shape[0]
    z = _in_proj(x, wts["g_mix"], wts["w_in"], _tile(n, 1024), 1024)
    cos, sin_lo, sin_hi = _rope_tables(pos_rows)
    q, k_f32, k_bf, v_bf = _qkv_prep(z, wts["g_q"], wts["g_k"], wts["bd"], cos, sin_lo, sin_hi,
                                     _tile(cos.shape[0], 256))
    att = attention(q, k_bf, v_bf)
    ys, s_re, s_im = ssm(z)
    merged = _merge(att, ys, z, wts["w_attn_up"], wts["w_glu"], _tile(n, 512), 512)
    x1, hn = _out_proj(x, merged, wts["w_out"], wts["g_ffn"], _tile(n, 256))
    a_t, b_t, g_t = _router(hn, wts["peer_w_query"], wts["peer_sub_keys"], 128)
    w = _wbuild(a_t.T, b_t.T, g_t.T, _tile(n, 64))
    peer = _experts(hn, wts["peer_u_t"], wts["peer_v"], w, _tile(n, 512))
    y = _ple(x1, peer, p, wts["w_ple"], wts["w_ple_gate"], wts["g_ple"], _tile(n, 256))
    k_new = k_f32.reshape(n, N_HEADS, 2 * HEAD_DK)
    v_new = z[:, 2 * QK_COLS:V_END].reshape(n, N_HEADS, HEAD_DV)
    return y, k_new, v_new, s_re, s_im


def kernel(x_prompt, x_sample, cache_k, cache_v, state_ssm_re, state_ssm_im, page_table, p_prompt, p_sample, g_mix, w_in, g_q, g_k, lambda_q, lambda_k, g_head, w_attn_up, ssm_a_re, ssm_a_im, ssm_log_dt, ssm_b_re, ssm_b_im, ssm_c_re, ssm_c_im, ssm_d, w_glu, w_out, g_ffn, peer_w_query, peer_sub_keys, peer_u, peer_v, g_ple, w_ple, w_ple_gate):
    depth = w_in.shape[0]
    assert depth == 1
    batch, seq, _ = x_prompt.shape
    n_dec, n_new, _ = x_sample.shape
    past_len = page_table.shape[1] * PAGE_SIZE
    i = 0
    lam_init = 0.8 - 0.6 * math.exp(-0.3 * i)

    row = lambda t: t.reshape(1, -1)
    bd = jnp.kron(jnp.eye(2, dtype=F32), jnp.ones((HEAD_DK, HEAD_DK), F32)).astype(BF16)
    wts = {
        "g_mix": row(g_mix[i]), "w_in": w_in[i].astype(BF16),
        "g_q": row(g_q[i]), "g_k": row(g_k[i]), "bd": bd,
        "w_attn_up": w_attn_up[i].astype(BF16), "w_glu": w_glu[i].astype(BF16),
        "w_out": w_out[i].astype(BF16), "g_ffn": row(g_ffn[i]),
        "peer_w_query": peer_w_query[i].astype(BF16), "peer_sub_keys": peer_sub_keys[i].astype(BF16),
        "peer_u_t": peer_u[i].T.astype(BF16), "peer_v": peer_v[i].astype(BF16),
        "g_ple": row(g_ple[i]), "w_ple": w_ple[i].astype(BF16), "w_ple_gate": w_ple_gate[i].astype(BF16),
    }
    g_head2 = row(g_head[i])
    lq, lk = lambda_q[i].astype(F32), lambda_k[i].astype(F32)
    lam = jnp.exp(jnp.sum(lq[0] * lk[0])) - jnp.exp(jnp.sum(lq[1] * lk[1])) + lam_init
    score_bound = (1.05 * QUERY_SCALE * HEAD_DK) * jnp.max(jnp.abs(g_q[i])) * jnp.max(jnp.abs(g_k[i]))
    lam = jnp.stack([lam, score_bound]).astype(F32)
    ssm_args = (ssm_a_re[i], ssm_a_im[i], ssm_log_dt[i], ssm_b_re[i], ssm_b_im[i],
                ssm_c_re[i], ssm_c_im[i], ssm_d[i])
    tables_prompt = tables_step = _ssm_tables(*ssm_args)

    pos_prompt = jnp.arange(seq, dtype=jnp.int32)
    y_p, k_p, v_p, sre_p, sim_p = _layer(
        x_prompt.reshape(batch * seq, D_MODEL), p_prompt[i].reshape(batch * seq, PLE_DIM), pos_prompt,
        lambda q, k, v: _flash_attention(q, k, v, g_head2, lam, lam_init, batch, seq, FLASH_BLOCK),
        lambda z: _ssm_prompt(z, tables_prompt, batch, seq),
        wts)

    pos_sample = jnp.tile(past_len + jnp.arange(n_new, dtype=jnp.int32), n_dec)
    y_s, k_s, v_s, sre_s, sim_s = _layer(
        x_sample.reshape(n_dec * n_new, D_MODEL), p_sample[i].reshape(n_dec * n_new, PLE_DIM), pos_sample,
        lambda q, k, v: _decode_attention(q, k, v, cache_k[i], cache_v[i], page_table, g_head2, lam,
                                          lam_init, n_dec, n_new),
        lambda z: _ssm_step(z, tables_step, state_ssm_re[i], state_ssm_im[i], n_dec, n_new),
        wts)

    lead = lambda t, *shape: t.reshape(1, *shape)
    return (y_p.reshape(batch, seq, D_MODEL), y_s.reshape(n_dec, n_new, D_MODEL),
            lead(k_p, batch, seq, N_HEADS, 2 * HEAD_DK), lead(v_p, batch, seq, N_HEADS, HEAD_DV),
            lead(sre_p, batch, SSM_GROUPS, SSM_STATE), lead(sim_p, batch, SSM_GROUPS, SSM_STATE),
            lead(k_s, n_dec, n_new, N_HEADS, 2 * HEAD_DK), lead(v_s, n_dec, n_new, N_HEADS, HEAD_DV),
            lead(sre_s, n_dec, SSM_GROUPS, SSM_STATE), lead(sim_s, n_dec, SSM_GROUPS, SSM_STATE))
```

```python
import functools
import math

import jax
import jax.numpy as jnp
from jax import lax
from jax.experimental import pallas as pl
from jax.experimental.pallas import tpu as pltpu

F32 = jnp.float32
BF16 = jnp.bfloat16

D_MODEL = 2048
PAGE_SIZE = 128
N_HEADS = 8
HEAD_DK = 64
HEAD_DV = 2 * HEAD_DK
QK_COLS = N_HEADS * 2 * HEAD_DK
ATTN_WIDTH = N_HEADS * HEAD_DV
ROPE_DIMS = HEAD_DK // 4
ROPE_THETA = 500000.0
NEG_INF = -1e30
SSM_WIDTH = D_MODEL // 2
SSM_GROUP = 16
SSM_GROUPS = SSM_WIDTH // SSM_GROUP
SSM_STATE = 64
PEER_HEADS = 8
PEER_KEYS = 128
PEER_EXPERTS = PEER_KEYS * PEER_KEYS
PEER_DKEY = 256
PEER_TOPK = 16
PLE_DIM = 256
RMS_EPS = 1e-6
V_END = 2 * QK_COLS + ATTN_WIDTH
U_END = V_END + SSM_WIDTH
IN_COLS = U_END + 2 * D_MODEL

LANES = 128
SUBLANES = 8
VMEM_LIMIT = 52 * 1024 * 1024
SSM_GB = LANES // SSM_GROUP
SSM_NB = SSM_GROUPS // SSM_GB
SSM_SW = SSM_GB * SSM_STATE
SSM_CHUNK = 512
SCAN_STEPS = (1, 2, 4)
DECODE_PAGES = 8
FLASH_BLOCK = 512
WBUILD_UNROLL = 8
EXPERT_TILE = 1024
QUERY_SCALE = HEAD_DK ** -0.5 * math.log2(math.e)
SAFE_SCORE_BOUND = 60.0


def _params(sem):
    return pltpu.CompilerParams(dimension_semantics=sem, vmem_limit_bytes=VMEM_LIMIT)


def _nt_dot(a, b):
    return lax.dot_general(a, b, (((1,), (1,)), ((), ())), preferred_element_type=F32)


def _in_proj_kernel(x_ref, g_ref, w_ref, o_ref, h_ref):
    @pl.when(pl.program_id(1) == 0)
    def _():
        x = x_ref[...]
        ms = jnp.mean(x * x, axis=-1, keepdims=True)
        h_ref[...] = (x * lax.rsqrt(ms + RMS_EPS) * g_ref[...]).astype(BF16)

    o_ref[...] = jnp.dot(h_ref[...], w_ref[...], preferred_element_type=F32)


def _in_proj(x, g, w_bf, tm, tn):
    n, d = x.shape
    cols = w_bf.shape[1]
    return pl.pallas_call(
        _in_proj_kernel,
        grid=(n // tm, cols // tn),
        in_specs=[
            pl.BlockSpec((tm, d), lambda i, j: (i, 0)),
            pl.BlockSpec((1, d), lambda i, j: (0, 0)),
            pl.BlockSpec((d, tn), lambda i, j: (0, j)),
        ],
        out_specs=pl.BlockSpec((tm, tn), lambda i, j: (i, j)),
        out_shape=jax.ShapeDtypeStruct((n, cols), F32),
        scratch_shapes=[pltpu.VMEM((tm, d), BF16)],
        compiler_params=_params(("parallel", "arbitrary")),
        name="in_proj",
    )(x, g, w_bf)


def _qkv_prep_kernel(zq_ref, zk_ref, zv_ref, gq_ref, gk_ref, bd_ref, c_ref, s1_ref, s2_ref,
                     q_ref, kf_ref, kb_ref, vb_ref):
    bd = bd_ref[...]
    cos, sin_lo, sin_hi = c_ref[...], s1_ref[...], s2_ref[...]

    def norm_rope(x, g):
        x2 = x * x
        hi = x2.astype(BF16)
        lo = (x2 - hi.astype(F32)).astype(BF16)
        ss = (jnp.dot(hi, bd, preferred_element_type=F32)
              + jnp.dot(lo, bd, preferred_element_type=F32))
        xn = x * lax.rsqrt(ss * (1.0 / HEAD_DK) + RMS_EPS) * g
        fwd = pltpu.roll(xn, LANES - ROPE_DIMS // 2, axis=1)
        bwd = pltpu.roll(xn, ROPE_DIMS // 2, axis=1)
        return xn * cos + fwd * sin_lo + bwd * sin_hi

    for c in range(QK_COLS // LANES):
        sl = slice(c * LANES, (c + 1) * LANES)
        q = norm_rope(zq_ref[:, sl], gq_ref[...])
        q_ref[:, sl] = (q * QUERY_SCALE).astype(BF16)
        k = norm_rope(zk_ref[:, sl], gk_ref[...])
        kf_ref[:, sl] = k
        kb_ref[:, sl] = k.astype(BF16)
    vb_ref[...] = zv_ref[...].astype(BF16)


def _qkv_prep(z, gq, gk, bd, cos, sin_lo, sin_hi, tm):
    n = z.shape[0]
    nt = cos.shape[0] // tm
    row = lambda c: pl.BlockSpec((tm, QK_COLS), lambda i: (i, c))
    vec = pl.BlockSpec((1, LANES), lambda i: (0, 0))
    tab = pl.BlockSpec((tm, LANES), lambda i: (i % nt, 0))
    out = pl.BlockSpec((tm, QK_COLS), lambda i: (i, 0))
    return pl.pallas_call(
        _qkv_prep_kernel,
        grid=(n // tm,),
        in_specs=[row(0), row(1), row(2), vec, vec,
                  pl.BlockSpec((LANES, LANES), lambda i: (0, 0)), tab, tab, tab],
        out_specs=[out, out, out, out],
        out_shape=[jax.ShapeDtypeStruct((n, QK_COLS), BF16),
                   jax.ShapeDtypeStruct((n, QK_COLS), F32),
                   jax.ShapeDtypeStruct((n, QK_COLS), BF16),
                   jax.ShapeDtypeStruct((n, ATTN_WIDTH), BF16)],
        compiler_params=_params(("parallel",)),
        name="qkv_prep",
    )(z, z, z, gq, gk, bd, cos, sin_lo, sin_hi)


def _rope_tables(pos):
    half = ROPE_DIMS // 2
    inv_freq = ROPE_THETA ** (-jnp.arange(half, dtype=F32) / half)
    ang = pos.astype(F32)[:, None] * inv_freq[None, :]
    cos, sin = jnp.cos(ang), jnp.sin(ang)
    ones = jnp.ones((pos.shape[0], HEAD_DK - ROPE_DIMS), F32)
    zeros8 = jnp.zeros_like(sin)
    zeros = jnp.zeros_like(ones)
    c = jnp.concatenate([cos, cos, ones], axis=1)
    s_lo = jnp.concatenate([-sin, zeros8, zeros], axis=1)
    s_hi = jnp.concatenate([zeros8, sin, zeros], axis=1)
    two = lambda a: jnp.concatenate([a, a], axis=1)
    return two(c), two(s_lo), two(s_hi)


def _head_finish(o, g, lam_init):
    ms = jnp.mean(o * o, axis=-1, keepdims=True)
    return o * lax.rsqrt(ms + RMS_EPS) * g * (1.0 - lam_init)


def _flash_kernel(lam_ref, q_ref, k_ref, v_ref, g_ref, o_ref, *, blk, lam_init):
    qi = pl.program_id(2)
    q = q_ref[0]
    lane = lax.broadcasted_iota(jnp.int32, q.shape, 1)
    zero = jnp.zeros_like(q)
    qq = jnp.concatenate([jnp.where(lane < HEAD_DK, q, zero),
                          jnp.where(lane >= HEAD_DK, q, zero)], axis=0)

    lam, bound = lam_ref[0], lam_ref[1]

    def scores(ki, masked):
        start = pl.multiple_of(ki * blk, blk)
        s = _nt_dot(qq, k_ref[0, pl.ds(start, blk), :])
        if masked:
            r = lax.broadcasted_iota(jnp.int32, s.shape, 0)
            c = lax.broadcasted_iota(jnp.int32, s.shape, 1)
            s = jnp.where(jnp.where(r >= blk, r - blk, r) >= c, s, NEG_INF)
        return s, v_ref[0, pl.ds(start, blk), :]

    def finish(l, acc):
        o = acc / l
        o = o[:blk] - lam * o[blk:]
        o_ref[0] = _head_finish(o, g_ref[...], lam_init).astype(BF16)

    zeros = (jnp.zeros((2 * blk, 1), F32), jnp.zeros((2 * blk, HEAD_DV), F32))

    @pl.when(bound <= SAFE_SCORE_BOUND)
    def _():
        def step(ki, carry, masked):
            l, acc = carry
            s, v = scores(ki, masked)
            p = jnp.exp2(s - bound)
            return (l + jnp.sum(p, axis=1, keepdims=True),
                    acc + jnp.dot(p.astype(BF16), v, preferred_element_type=F32))

        carry = lax.fori_loop(0, qi, lambda ki, c: step(ki, c, False), zeros)
        finish(*step(qi, carry, True))

    @pl.when(bound > SAFE_SCORE_BOUND)
    def _():
        def step(ki, carry, masked):
            m, l, acc = carry
            s, v = scores(ki, masked)
            m_new = jnp.maximum(m, jnp.max(s, axis=1, keepdims=True))
            alpha = jnp.exp2(m - m_new)
            p = jnp.exp2(s - m_new)
            return (m_new, alpha * l + jnp.sum(p, axis=1, keepdims=True),
                    alpha * acc + jnp.dot(p.astype(BF16), v, preferred_element_type=F32))

        init = (jnp.full((2 * blk, 1), NEG_INF, F32),) + zeros
        carry = lax.fori_loop(0, qi, lambda ki, c: step(ki, c, False), init)
        finish(*step(qi, carry, True)[1:])


def _flash_attention(q, k, v, g_head, lam, lam_init, batch, seq, blk):
    q3, k3, v3 = (t.reshape(batch, seq, QK_COLS) for t in (q, k, v))
    whole = pl.BlockSpec((1, seq, LANES), lambda b, h, i: (b, 0, h))
    out = pl.pallas_call(
        functools.partial(_flash_kernel, blk=blk, lam_init=lam_init),
        grid=(batch, N_HEADS, seq // blk),
        in_specs=[
            pl.BlockSpec(memory_space=pltpu.SMEM),
            pl.BlockSpec((1, blk, LANES), lambda b, h, i: (b, i, h)),
            whole, whole,
            pl.BlockSpec((1, LANES), lambda b, h, i: (0, 0)),
        ],
        out_specs=pl.BlockSpec((1, blk, LANES), lambda b, h, i: (b, i, h)),
        out_shape=jax.ShapeDtypeStruct((batch, seq, ATTN_WIDTH), BF16),
        compiler_params=_params(("parallel", "parallel", "arbitrary")),
        name="flash_attention",
    )(lam, q3, k3, v3, g_head)
    return out.reshape(batch * seq, ATTN_WIDTH)


def _decode_kernel(pt_ref, lam_ref, qt_ref, bias_ref, biasn_ref, kn_ref, vn_ref, g_ref, *rest,
                   n_rows, lam_init):
    k_refs, v_refs = rest[:DECODE_PAGES], rest[DECODE_PAGES:2 * DECODE_PAGES]
    o_ref, m_ref, l_ref, acc_ref, s_ref = rest[2 * DECODE_PAGES:]
    step = pl.program_id(1)
    is_last = step == pl.num_programs(1) - 1
    page_rows = PAGE_SIZE * N_HEADS

    @pl.when(step == 0)
    def _():
        m_ref[...] = jnp.full_like(m_ref, NEG_INF)
        l_ref[...] = jnp.zeros_like(l_ref)
        acc_ref[...] = jnp.zeros_like(acc_ref)

    eye = (lax.broadcasted_iota(jnp.int32, (LANES, LANES), 0)
           == lax.broadcasted_iota(jnp.int32, (LANES, LANES), 1))

    def to_col(row):
        return jnp.sum(jnp.where(eye, jnp.broadcast_to(row, (LANES, LANES)), 0.0), axis=1, keepdims=True)

    def contract_rows(p, v):
        return lax.dot_general(p.astype(BF16), v, (((0,), (0,)), ((), ())), preferred_element_type=F32)

    qt = qt_ref[0]
    bias = bias_ref[...]
    m_prev = m_ref[...]
    m_new = m_prev
    for r in range(DECODE_PAGES):
        k2d = k_refs[r][...].reshape(page_rows, LANES).astype(BF16)
        s = jnp.dot(k2d, qt, preferred_element_type=F32) + bias
        s_ref[r] = s
        m_new = jnp.maximum(m_new, jnp.max(s, axis=0, keepdims=True))
    s_new = (jnp.dot(kn_ref[0], qt, preferred_element_type=F32) + biasn_ref[...]
             + jnp.where(is_last, 0.0, NEG_INF))
    m_new = jnp.maximum(m_new, jnp.max(s_new, axis=0, keepdims=True))

    alpha = jnp.exp2(m_prev - m_new)
    p_new = jnp.exp2(s_new - m_new)
    l = alpha * l_ref[...] + jnp.sum(p_new, axis=0, keepdims=True)
    pv = contract_rows(p_new, vn_ref[0])
    for r in range(DECODE_PAGES):
        p = jnp.exp2(s_ref[r] - m_new)
        l = l + jnp.sum(p, axis=0, keepdims=True)
        pv = pv + contract_rows(p, v_refs[r][...].reshape(page_rows, LANES).astype(BF16))
    acc = to_col(alpha) * acc_ref[...] + pv
    acc_ref[...] = acc
    m_ref[...] = m_new
    l_ref[...] = l

    @pl.when(is_last)
    def _():
        o = acc / to_col(l)
        o = o[:n_rows] - lam_ref[0] * o[n_rows:2 * n_rows]
        o_ref[0] = _head_finish(o, g_ref[...], lam_init).astype(BF16)


def _decode_attention(q, k_new, v_new, cache_k, cache_v, page_table, g_head, lam, lam_init, n_dec, n_new):
    n_pages = page_table.shape[1]
    n_rows = N_HEADS * n_new
    q5 = q.reshape(n_dec, n_new, N_HEADS, 2, HEAD_DK)
    qt = jnp.einsum("bthmd,mM->bMdmht", q5, jnp.eye(2, dtype=BF16)).reshape(n_dec, LANES, 2 * n_rows)
    qt = jnp.pad(qt, ((0, 0), (0, 0), (0, LANES - 2 * n_rows)))
    col = jnp.arange(LANES)
    col_ok, col_head, col_t = col < 2 * n_rows, (col % n_rows) // n_new, col % n_new
    row = jnp.arange(PAGE_SIZE * N_HEADS)
    bias = jnp.where(col_ok[None] & (row[:, None] % N_HEADS == col_head[None]), 0.0, NEG_INF).astype(F32)
    row_n = jnp.arange(n_rows)
    bias_new = jnp.where(col_ok[None] & (row_n[:, None] % N_HEADS == col_head[None])
                         & (row_n[:, None] // N_HEADS <= col_t[None]), 0.0, NEG_INF).astype(F32)
    kn = k_new.reshape(n_dec, n_rows, LANES)
    vn = v_new.reshape(n_dec, n_rows, LANES)

    def page_spec(r):
        return pl.BlockSpec((None, PAGE_SIZE, N_HEADS, LANES),
                            lambda b, s, pt: (pt[b * n_pages + s * DECODE_PAGES + r], 0, 0, 0))

    const = lambda shape: pl.BlockSpec(shape, lambda b, s, pt: (0,) * len(shape))
    per_b = lambda shape: pl.BlockSpec(shape, lambda b, s, pt: (b,) + (0,) * (len(shape) - 1))
    pages = [page_spec(r) for r in range(DECODE_PAGES)]
    grid_spec = pltpu.PrefetchScalarGridSpec(
        num_scalar_prefetch=1,
        grid=(n_dec, n_pages // DECODE_PAGES),
        in_specs=[pl.BlockSpec(memory_space=pltpu.SMEM), per_b((1, LANES, LANES)),
                  const((PAGE_SIZE * N_HEADS, LANES)), const((n_rows, LANES)),
                  per_b((1, n_rows, LANES)), per_b((1, n_rows, LANES)), const((1, LANES))]
                 + pages + pages,
        out_specs=per_b((1, n_rows, HEAD_DV)),
        scratch_shapes=[pltpu.VMEM((1, LANES), F32), pltpu.VMEM((1, LANES), F32),
                        pltpu.VMEM((LANES, HEAD_DV), F32),
                        pltpu.VMEM((DECODE_PAGES, PAGE_SIZE * N_HEADS, LANES), F32)],
    )
    out = pl.pallas_call(
        functools.partial(_decode_kernel, n_rows=n_rows, lam_init=lam_init),
        grid_spec=grid_spec,
        out_shape=jax.ShapeDtypeStruct((n_dec, n_rows, HEAD_DV), BF16),
        compiler_params=_params(("parallel", "arbitrary")),
        name="decode_attention",
    )(page_table.reshape(-1), lam, qt, bias, bias_new, kn, vn, g_head,
      *([cache_k] * DECODE_PAGES), *([cache_v] * DECODE_PAGES))
    out = out.reshape(n_dec, N_HEADS, n_new, HEAD_DV).transpose(0, 2, 1, 3)
    return out.reshape(n_dec * n_new, ATTN_WIDTH)


def _ssm_tables(a_re, a_im, log_dt, b_re, b_im, c_re, c_im, d_skip):
    dt = jnp.exp(log_dt)[:, None]
    mag = jnp.exp(a_re * dt)
    ab_re, ab_im = mag * jnp.cos(a_im * dt), mag * jnp.sin(a_im * dt)
    den = a_re * a_re + a_im * a_im
    f_re = ((ab_re - 1.0) * a_re + ab_im * a_im) / den
    f_im = (ab_im * a_re - (ab_re - 1.0) * a_im) / den
    bb_re = f_re[..., None] * b_re - f_im[..., None] * b_im
    bb_im = f_re[..., None] * b_im + f_im[..., None] * b_re
    eye = jnp.eye(SSM_GB, dtype=F32)

    def b_blocks(bb):
        bb = bb.reshape(SSM_NB, SSM_GB, SSM_STATE, SSM_GROUP)
        return jnp.einsum("agpc,gh->agchp", bb, eye).reshape(SSM_NB, LANES, SSM_SW)

    def c_blocks(cc):
        cc = cc.reshape(SSM_NB, SSM_GB, SSM_GROUP, SSM_STATE)
        return jnp.einsum("agcp,gh->agphc", cc, eye).reshape(SSM_NB, SSM_SW, LANES)

    bbd = jnp.concatenate([b_blocks(bb_re), b_blocks(bb_im)], axis=2).astype(BF16)
    cbd = jnp.concatenate([c_blocks(c_re), c_blocks(-c_im)], axis=1).astype(BF16)
    d = d_skip.reshape(SSM_NB, 1, LANES)
    pr, pi = ab_re[None], ab_im[None]
    while pr.shape[0] < SUBLANES:
        tr, ti = pr[-1:], pi[-1:]
        pr, pi = (jnp.concatenate([pr, pr * tr - pi * ti], axis=0),
                  jnp.concatenate([pi, pr * ti + pi * tr], axis=0))
    lay = lambda t: t.reshape(t.shape[0], SSM_NB, SSM_SW).transpose(1, 0, 2)
    apow = jnp.concatenate([lay(pr), lay(pi)], axis=2)
    rows = jnp.arange(SUBLANES)[None, :, None]
    steps = [jnp.where(rows >= dd, apow[:, dd - 1:dd, :], 0.0) for dd in SCAN_STEPS]
    amask = jnp.stack(steps, axis=1)
    return bbd, cbd, d, apow, amask


def _ssm_prompt_kernel(u_ref, b_ref, c_ref, d_ref, ap_ref, am_ref, y_ref, sre_ref, sim_ref, h_ref):
    chunk = pl.program_id(2)

    @pl.when(chunk == 0)
    def _():
        h_ref[...] = jnp.zeros_like(h_ref)

    u = u_ref[0]
    length = u.shape[0]
    bu = jnp.dot(u.astype(BF16), b_ref[0], preferred_element_type=F32)
    pr, pi = ap_ref[0, :, :SSM_SW], ap_ref[0, :, SSM_SW:]
    cr, ci = h_ref[0:1, :SSM_SW], h_ref[0:1, SSM_SW:]
    out_re, out_im = [], []
    for j in range(length // SUBLANES):
        rows = slice(j * SUBLANES, (j + 1) * SUBLANES)
        re, im = bu[rows, :SSM_SW], bu[rows, SSM_SW:]
        for si, dd in enumerate(SCAN_STEPS):
            mr, mi = am_ref[0, si, :, :SSM_SW], am_ref[0, si, :, SSM_SW:]
            sr, s_i = pltpu.roll(re, dd, axis=0), pltpu.roll(im, dd, axis=0)
            re, im = re + mr * sr - mi * s_i, im + mr * s_i + mi * sr
        re, im = re + pr * cr - pi * ci, im + pr * ci + pi * cr
        cr, ci = re[SUBLANES - 1:SUBLANES], im[SUBLANES - 1:SUBLANES]
        out_re.append(re)
        out_im.append(im)
    h_ref[0:1, :SSM_SW] = cr
    h_ref[0:1, SSM_SW:] = ci
    hcat = jnp.concatenate([jnp.concatenate(out_re, axis=0), jnp.concatenate(out_im, axis=0)],
                           axis=1).astype(BF16)
    y = jnp.dot(hcat, c_ref[0], preferred_element_type=F32) + d_ref[0] * u
    y_ref[0] = jax.nn.gelu(y).astype(BF16)

    @pl.when(chunk == pl.num_programs(2) - 1)
    def _():
        sre_ref[0, 0] = cr
        sim_ref[0, 0] = ci


def _ssm_prompt(z, tables, batch, seq):
    bbd, cbd, d, apow, amask = tables
    length = SSM_CHUNK
    z3 = z.reshape(batch, seq, IN_COLS)
    u_blk = V_END // LANES
    y, sre, sim = pl.pallas_call(
        _ssm_prompt_kernel,
        grid=(batch, SSM_NB, seq // length),
        in_specs=[
            pl.BlockSpec((1, length, LANES), lambda b, g, c: (b, c, u_blk + g)),
            pl.BlockSpec((1, LANES, 2 * SSM_SW), lambda b, g, c: (g, 0, 0)),
            pl.BlockSpec((1, 2 * SSM_SW, LANES), lambda b, g, c: (g, 0, 0)),
            pl.BlockSpec((1, 1, LANES), lambda b, g, c: (g, 0, 0)),
            pl.BlockSpec((1, SUBLANES, 2 * SSM_SW), lambda b, g, c: (g, 0, 0)),
            pl.BlockSpec((1, len(SCAN_STEPS), SUBLANES, 2 * SSM_SW), lambda b, g, c: (g, 0, 0, 0)),
        ],
        out_specs=[
            pl.BlockSpec((1, length, LANES), lambda b, g, c: (b, c, g)),
            pl.BlockSpec((1, 1, 1, SSM_SW), lambda b, g, c: (b, g, 0, 0)),
            pl.BlockSpec((1, 1, 1, SSM_SW), lambda b, g, c: (b, g, 0, 0)),
        ],
        out_shape=[jax.ShapeDtypeStruct((batch, seq, SSM_WIDTH), BF16),
                   jax.ShapeDtypeStruct((batch, SSM_NB, 1, SSM_SW), F32),
                   jax.ShapeDtypeStruct((batch, SSM_NB, 1, SSM_SW), F32)],
        scratch_shapes=[pltpu.VMEM((SUBLANES, 2 * SSM_SW), F32)],
        compiler_params=_params(("parallel", "parallel", "arbitrary")),
        name="ssm_prompt",
    )(z3, bbd, cbd, d, apow, amask)
    state = lambda s: s.reshape(batch, SSM_GROUPS, SSM_STATE)
    return y.reshape(batch * seq, SSM_WIDTH), state(sre), state(sim)


def _ssm_step_kernel(u_ref, b_ref, c_ref, d_ref, ap_ref, hre_ref, him_ref, y_ref, sre_ref, sim_ref):
    ar, ai = ap_ref[0, 0:1, :SSM_SW], ap_ref[0, 0:1, SSM_SW:]
    re, im = hre_ref[...], him_ref[...]
    bmat = b_ref[0]
    for t in range(u_ref.shape[0]):
        u = u_ref[t]
        u_hi = u.astype(BF16)
        u_lo = (u - u_hi.astype(F32)).astype(BF16)
        bu = (jnp.dot(u_hi, bmat, preferred_element_type=F32)
              + jnp.dot(u_lo, bmat, preferred_element_type=F32))
        re, im = (ar * re - ai * im + bu[:, :SSM_SW], ar * im + ai * re + bu[:, SSM_SW:])
        hcat = jnp.concatenate([re, im], axis=1).astype(BF16)
        y = jnp.dot(hcat, c_ref[0], preferred_element_type=F32) + d_ref[0] * u
        y_ref[t] = jax.nn.gelu(y).astype(BF16)
    sre_ref[...] = re
    sim_ref[...] = im


def _ssm_step(z, tables, h_re, h_im, batch, steps):
    bbd, cbd, d, apow, _ = tables
    u = z[:, V_END:U_END].reshape(batch, steps, SSM_WIDTH).transpose(1, 0, 2)
    flat = lambda s: s.reshape(batch, SSM_GROUPS * SSM_STATE)
    y, sre, sim = pl.pallas_call(
        _ssm_step_kernel,
        grid=(SSM_NB,),
        in_specs=[
            pl.BlockSpec((steps, batch, LANES), lambda g: (0, 0, g)),
            pl.BlockSpec((1, LANES, 2 * SSM_SW), lambda g: (g, 0, 0)),
            pl.BlockSpec((1, 2 * SSM_SW, LANES), lambda g: (g, 0, 0)),
            pl.BlockSpec((1, 1, LANES), lambda g: (g, 0, 0)),
            pl.BlockSpec((1, SUBLANES, 2 * SSM_SW), lambda g: (g, 0, 0)),
            pl.BlockSpec((batch, SSM_SW), lambda g: (0, g)),
            pl.BlockSpec((batch, SSM_SW), lambda g: (0, g)),
        ],
        out_specs=[
            pl.BlockSpec((steps, batch, LANES), lambda g: (0, 0, g)),
            pl.BlockSpec((batch, SSM_SW), lambda g: (0, g)),
            pl.BlockSpec((batch, SSM_SW), lambda g: (0, g)),
        ],
        out_shape=[jax.ShapeDtypeStruct((steps, batch, SSM_WIDTH), BF16),
                   jax.ShapeDtypeStruct((batch, SSM_GROUPS * SSM_STATE), F32),
                   jax.ShapeDtypeStruct((batch, SSM_GROUPS * SSM_STATE), F32)],
        compiler_params=_params(("parallel",)),
        name="ssm_step",
    )(u, bbd, cbd, d, apow, flat(h_re), flat(h_im))
    y = y.transpose(1, 0, 2).reshape(batch * steps, SSM_WIDTH)
    state = lambda s: s.reshape(batch, SSM_GROUPS, SSM_STATE)
    return y, state(sre), state(sim)


def _merge_kernel(att_ref, ys_ref, za_ref, zs_ref, wup_ref, wga_ref, wgb_ref, o_ref):
    ys = ys_ref[...]
    branch_a = jnp.dot(att_ref[...], wup_ref[...], preferred_element_type=F32)
    glu_a = jnp.dot(ys, wga_ref[...], preferred_element_type=F32)
    glu_b = jnp.dot(ys, wgb_ref[...], preferred_element_type=F32)
    branch_s = glu_a * jax.nn.sigmoid(glu_b)
    merged = jax.nn.sigmoid(za_ref[...]) * branch_a + jax.nn.sigmoid(zs_ref[...]) * branch_s
    o_ref[...] = merged.astype(BF16)


def _merge(att, ys, z, w_up, w_glu, tm, tn):
    n = att.shape[0]
    nj = D_MODEL // tn
    act = pl.BlockSpec((tm, ATTN_WIDTH), lambda j, i: (i, 0))
    gate = lambda off: pl.BlockSpec((tm, tn), lambda j, i: (i, off + j))
    wcol = lambda off: pl.BlockSpec((ATTN_WIDTH, tn), lambda j, i: (0, off + j))
    return pl.pallas_call(
        _merge_kernel,
        grid=(nj, n // tm),
        in_specs=[act, act, gate(U_END // tn), gate((U_END + D_MODEL) // tn),
                  wcol(0), wcol(0), wcol(nj)],
        out_specs=pl.BlockSpec((tm, tn), lambda j, i: (i, j)),
        out_shape=jax.ShapeDtypeStruct((n, D_MODEL), BF16),
        compiler_params=_params(("parallel", "parallel")),
        name="merge",
    )(att, ys, z, z, w_up, w_glu, w_glu)


def _out_proj_kernel(x_ref, m_ref, w_ref, g_ref, x1_ref, hn_ref):
    x1 = x_ref[...] + jnp.dot(m_ref[...], w_ref[...], preferred_element_type=F32)
    x1_ref[...] = x1
    ms = jnp.mean(x1 * x1, axis=-1, keepdims=True)
    hn_ref[...] = (x1 * lax.rsqrt(ms + RMS_EPS) * g_ref[...]).astype(BF16)


def _out_proj(x, merged, w_out, g_ffn, tm):
    n = x.shape[0]
    row = pl.BlockSpec((tm, D_MODEL), lambda i: (i, 0))
    return pl.pallas_call(
        _out_proj_kernel,
        grid=(n // tm,),
        in_specs=[row, row, pl.BlockSpec((D_MODEL, D_MODEL), lambda i: (0, 0)),
                  pl.BlockSpec((1, D_MODEL), lambda i: (0, 0))],
        out_specs=[row, row],
        out_shape=[jax.ShapeDtypeStruct((n, D_MODEL), F32),
                   jax.ShapeDtypeStruct((n, D_MODEL), BF16)],
        compiler_params=_params(("parallel",)),
        name="out_proj",
    )(x, merged, w_out, g_ffn)


def _top_rows(s, k):
    n = s.shape[0]
    row = lax.broadcasted_iota(jnp.int32, s.shape, 0)
    vals, idxs = [], []
    for _ in range(k):
        best = jnp.max(s, axis=0, keepdims=True)
        pick = jnp.min(jnp.where(s == best, row, n), axis=0, keepdims=True)
        vals.append(best)
        idxs.append(pick)
        s = jnp.where(row == pick, -jnp.inf, s)
    return jnp.concatenate(vals, axis=0), jnp.concatenate(idxs, axis=0)


def _router_kernel(h_ref, wq_ref, sk_ref, a_ref, b_ref, g_ref):
    tm = h_ref.shape[0]
    k = PEER_TOPK
    q = jnp.dot(h_ref[...], wq_ref[...], preferred_element_type=F32).astype(BF16)
    half = PEER_DKEY // 2

    n_rows = k + 8 * SUBLANES
    r = lax.broadcasted_iota(jnp.int32, (n_rows, tm), 0)
    p_mid = ((r - k) >> 3) + 1
    q_mid = (r - k) & 7
    valid = (r < k) | (r >= k + 7 * SUBLANES) | ((p_mid + 1) * (q_mid + 1) <= k)
    r16 = lax.broadcasted_iota(jnp.int32, (k, tm), 0)

    def cand_rows(first, second):
        parts = [first[0:1] + second[0:k]]
        parts += [first[p:p + 1] + second[0:SUBLANES] for p in range(1, 8)]
        parts.append(first[8:16] + second[0:1])
        return jnp.concatenate(parts, axis=0)

    for h in range(PEER_HEADS):
        tops = []
        for m in range(2):
            c = 2 * h + m
            s = _nt_dot(sk_ref[m], q[:, c * half:(c + 1) * half])
            tops.append(_top_rows(s, k))
        (v1, i1), (v2, i2) = tops
        cand = jnp.where(valid, cand_rows(v1, v2), -jnp.inf)
        fv, frow = _top_rows(cand, k)
        tail = k + 7 * SUBLANES
        p_pos = jnp.where(frow < k, 0, jnp.where(frow >= tail, frow - tail + SUBLANES, ((frow - k) >> 3) + 1))
        q_pos = jnp.where(frow < k, frow, jnp.where(frow >= tail, 0, (frow - k) & 7))
        sel_a, sel_b = [], []
        for j in range(k):
            sel_a.append(jnp.max(jnp.where(r16 == p_pos[j:j + 1], i1, -1), axis=0, keepdims=True))
            sel_b.append(jnp.max(jnp.where(r16 == q_pos[j:j + 1], i2, -1), axis=0, keepdims=True))
        e = jnp.exp(fv - fv[0:1])
        gate = e / jnp.sum(e, axis=0, keepdims=True)
        a_ref[h * k:(h + 1) * k, :] = jnp.concatenate(sel_a, axis=0)
        b_ref[h * k:(h + 1) * k, :] = jnp.concatenate(sel_b, axis=0)
        g_ref[h * k:(h + 1) * k, :] = gate


def _router(hn, w_query, sub_keys, tm):
    n = hn.shape[0]
    slots = PEER_HEADS * PEER_TOPK
    out = pl.BlockSpec((slots, tm), lambda i: (0, i))
    return pl.pallas_call(
        _router_kernel,
        grid=(n // tm,),
        in_specs=[pl.BlockSpec((tm, D_MODEL), lambda i: (i, 0)),
                  pl.BlockSpec((D_MODEL, PEER_HEADS * PEER_DKEY), lambda i: (0, 0)),
                  pl.BlockSpec((2, PEER_KEYS, PEER_DKEY // 2), lambda i: (0, 0, 0))],
        out_specs=[out, out, out],
        out_shape=[jax.ShapeDtypeStruct((slots, n), jnp.int32),
                   jax.ShapeDtypeStruct((slots, n), jnp.int32),
                   jax.ShapeDtypeStruct((slots, n), F32)],
        compiler_params=_params(("parallel",)),
        name="router",
    )(hn, w_query, sub_keys)


def _wbuild_kernel(a_ref, b_ref, g_ref, w_ref):
    tm = a_ref.shape[0]
    idx = lax.broadcasted_iota(jnp.int32, (PEER_KEYS, a_ref.shape[1]), 0)
    grouped = (PEER_KEYS // SUBLANES, SUBLANES, PEER_KEYS)
    sub = lax.broadcasted_iota(jnp.int32, grouped, 1)

    def body(i, carry):
        base = pl.multiple_of(i * WBUILD_UNROLL, WBUILD_UNROLL)
        a_rows = a_ref[pl.ds(base, WBUILD_UNROLL), :]
        b_rows = b_ref[pl.ds(base, WBUILD_UNROLL), :]
        g_rows = g_ref[pl.ds(base, WBUILD_UNROLL), :]
        planes = []
        for t in range(WBUILD_UNROLL):
            first = jnp.where(a_rows[t:t + 1] == idx, g_rows[t:t + 1], 0.0).astype(BF16)
            second = jnp.where(b_rows[t:t + 1] == idx, 1.0, 0.0).astype(BF16)
            planes.append(_nt_dot(first, second).reshape(grouped))
        for d in (4, 2, 1):
            upper = (sub & d) != 0
            for t in range(WBUILD_UNROLL):
                if t & d == 0:
                    lo, hi = planes[t], planes[t + d]
                    planes[t] = jnp.where(upper, pltpu.roll(hi, d, axis=1), lo)
                    planes[t + d] = jnp.where(upper, hi, pltpu.roll(lo, SUBLANES - d, axis=1))
        for r in range(WBUILD_UNROLL):
            w_ref[i, :, r] = planes[r]
        return carry

    lax.fori_loop(0, tm // WBUILD_UNROLL, body, 0)


def _wbuild(a, b, g, tm):
    n, slots = a.shape
    row = pl.BlockSpec((tm, slots), lambda i: (i, 0))
    key_hi = PEER_KEYS // SUBLANES
    w = pl.pallas_call(
        _wbuild_kernel,
        grid=(n // tm,),
        in_specs=[row, row, row],
        out_specs=pl.BlockSpec((tm // WBUILD_UNROLL, key_hi, SUBLANES, WBUILD_UNROLL, PEER_KEYS),
                               lambda i: (i, 0, 0, 0, 0)),
        out_shape=jax.ShapeDtypeStruct((n // WBUILD_UNROLL, key_hi, SUBLANES, WBUILD_UNROLL, PEER_KEYS), F32),
        compiler_params=_params(("parallel",)),
        name="wbuild",
    )(a, b, g)
    return w.reshape(n // WBUILD_UNROLL, PEER_KEYS, WBUILD_UNROLL, PEER_KEYS)


def _experts_kernel(h_ref, u_ref, v_ref, w_ref, o_ref):
    @pl.when(pl.program_id(1) == 0)
    def _():
        o_ref[...] = jnp.zeros_like(o_ref)

    tm = h_ref.shape[0]
    s = _nt_dot(h_ref[...], u_ref[...])
    w = jnp.concatenate([w_ref[:, r].reshape(tm, PEER_KEYS) for r in range(w_ref.shape[1])], axis=1)
    c = (jax.nn.gelu(s) * w).astype(BF16)
    o_ref[...] += jnp.dot(c, v_ref[...], preferred_element_type=F32)


def _experts(hn, u, v, w, tm):
    n = hn.shape[0]
    keys_per_tile = EXPERT_TILE // PEER_KEYS
    return pl.pallas_call(
        _experts_kernel,
        grid=(n // tm, PEER_EXPERTS // EXPERT_TILE),
        in_specs=[pl.BlockSpec((tm, D_MODEL), lambda i, j: (i, 0)),
                  pl.BlockSpec((EXPERT_TILE, D_MODEL), lambda i, j: (j, 0)),
                  pl.BlockSpec((EXPERT_TILE, D_MODEL), lambda i, j: (j, 0)),
                  pl.BlockSpec((tm // WBUILD_UNROLL, keys_per_tile, WBUILD_UNROLL, PEER_KEYS),
                               lambda i, j: (i, j, 0, 0))],
        out_specs=pl.BlockSpec((tm, D_MODEL), lambda i, j: (i, 0)),
        out_shape=jax.ShapeDtypeStruct((n, D_MODEL), F32),
        compiler_params=_params(("parallel", "arbitrary")),
        name="experts",
    )(hn, u, v, w)


def _ple_kernel(x1_ref, peer_ref, p_ref, wp_ref, wg_ref, g_ref, o_ref):
    x2 = x1_ref[...] + peer_ref[...]
    ms = jnp.mean(x2 * x2, axis=-1, keepdims=True)
    hn = (x2 * lax.rsqrt(ms + RMS_EPS) * g_ref[...]).astype(BF16)
    emb = jnp.dot(p_ref[...].astype(BF16), wp_ref[...], preferred_element_type=F32)
    gate = jax.nn.sigmoid(jnp.dot(hn, wg_ref[...], preferred_element_type=F32))
    o_ref[...] = x2 + emb * gate


def _ple(x1, peer, p, w_ple, w_gate, g_ple, tm):
    n = x1.shape[0]
    row = pl.BlockSpec((tm, D_MODEL), lambda i: (i, 0))
    return pl.pallas_call(
        _ple_kernel,
        grid=(n // tm,),
        in_specs=[row, row, pl.BlockSpec((tm, PLE_DIM), lambda i: (i, 0)),
                  pl.BlockSpec((PLE_DIM, D_MODEL), lambda i: (0, 0)),
                  pl.BlockSpec((D_MODEL, D_MODEL), lambda i: (0, 0)),
                  pl.BlockSpec((1, D_MODEL), lambda i: (0, 0))],
        out_specs=row,
        out_shape=jax.ShapeDtypeStruct((n, D_MODEL), F32),
        compiler_params=_params(("parallel",)),
        name="ple",
    )(x1, peer, p, w_ple, w_gate, g_ple)


def _tile(n, pref):
    return pref if n % pref == 0 else n


def _layer(x, p, pos_rows, attention, ssm, wts):
    n = x.shape[0]
    z = _in_proj(x, wts["g_mix"], wts["w_in"], _tile(n, 1024), 1024)
    cos, sin_lo, sin_hi = _rope_tables(pos_rows)
    q, k_f32, k_bf, v_bf = _qkv_prep(z, wts["g_q"], wts["g_k"], wts["bd"], cos, sin_lo, sin_hi,
                                     _tile(cos.shape[0], 256))
    att = attention(q, k_bf, v_bf)
    ys, s_re, s_im = ssm(z)
    merged = _merge(att, ys, z, wts["w_attn_up"], wts["w_glu"], _tile(n, 512), 512)
    x1, hn = _out_proj(x, merged, wts["w_out"], wts["g_ffn"], _tile(n, 256))
    a_t, b_t, g_t = _router(hn, wts["peer_w_query"], wts["peer_sub_keys"], 128)
    w = _wbuild(a_t.T, b_t.T, g_t.T, _tile(n, 64))
    peer = _experts(hn, wts["peer_u"], wts["peer_v"], w, _tile(n, 512))
    y = _ple(x1, peer, p, wts["w_ple"], wts["w_ple_gate"], wts["g_ple"], _tile(n, 256))
    k_new = k_f32.reshape(n, N_HEADS, 2 * HEAD_DK)
    v_new = z[:, 2 * QK_COLS:V_END].reshape(n, N_HEADS, HEAD_DV)
    return y, k_new, v_new, s_re, s_im


def kernel(x_prompt, x_sample, cache_k, cache_v, state_ssm_re, state_ssm_im, page_table, p_prompt, p_sample, g_mix, w_in, g_q, g_k, lambda_q, lambda_k, g_head, w_attn_up, ssm_a_re, ssm_a_im, ssm_log_dt, ssm_b_re, ssm_b_im, ssm_c_re, ssm_c_im, ssm_d, w_glu, w_out, g_ffn, peer_w_query, peer_sub_keys, peer_u, peer_v, g_ple, w_ple, w_ple_gate):
    depth = w_in.shape[0]
    assert depth == 1
    batch, seq, _ = x_prompt.shape
    n_dec, n_new, _ = x_sample.shape
    past_len = page_table.shape[1] * PAGE_SIZE
    i = 0
    lam_init = 0.8 - 0.6 * math.exp(-0.3 * i)

    row = lambda t: t.reshape(1, -1)
    bd = jnp.kron(jnp.eye(2, dtype=F32), jnp.ones((HEAD_DK, HEAD_DK), F32)).astype(BF16)
    wts = {
        "g_mix": row(g_mix[i]), "w_in": w_in[i].astype(BF16),
        "g_q": row(g_q[i]), "g_k": row(g_k[i]), "bd": bd,
        "w_attn_up": w_attn_up[i].astype(BF16), "w_glu": w_glu[i].astype(BF16),
        "w_out": w_out[i].astype(BF16), "g_ffn": row(g_ffn[i]),
        "peer_w_query": peer_w_query[i].astype(BF16), "peer_sub_keys": peer_sub_keys[i].astype(BF16),
        "peer_u": peer_u[i].astype(BF16), "peer_v": peer_v[i].astype(BF16),
        "g_ple": row(g_ple[i]), "w_ple": w_ple[i].astype(BF16), "w_ple_gate": w_ple_gate[i].astype(BF16),
    }
    g_head2 = row(g_head[i])
    lq, lk = lambda_q[i].astype(F32), lambda_k[i].astype(F32)
    lam = jnp.exp(jnp.sum(lq[0] * lk[0])) - jnp.exp(jnp.sum(lq[1] * lk[1])) + lam_init
    score_bound = (1.05 * QUERY_SCALE * HEAD_DK) * jnp.max(jnp.abs(g_q[i])) * jnp.max(jnp.abs(g_k[i]))
    lam = jnp.stack([lam, score_bound]).astype(F32)
    ssm_args = (ssm_a_re[i], ssm_a_im[i], ssm_log_dt[i], ssm_b_re[i], ssm_b_im[i],
                ssm_c_re[i], ssm_c_im[i], ssm_d[i])
    tables_prompt = tables_step = _ssm_tables(*ssm_args)

    pos_prompt = jnp.arange(seq, dtype=jnp.int32)
    y_p, k_p, v_p, sre_p, sim_p = _layer(
        x_prompt.reshape(batch * seq, D_MODEL), p_prompt[i].reshape(batch * seq, PLE_DIM), pos_prompt,
        lambda q, k, v: _flash_attention(q, k, v, g_head2, lam, lam_init, batch, seq, FLASH_BLOCK),
        lambda z: _ssm_prompt(z, tables_prompt, batch, seq),
        wts)

    pos_sample = jnp.tile(past_len + jnp.arange(n_new, dtype=jnp.int32), n_dec)
    y_s, k_s, v_s, sre_s, sim_s = _layer(
        x_sample.reshape(n_dec * n_new, D_MODEL), p_sample[i].reshape(n_dec * n_new, PLE_DIM), pos_sample,
        lambda q, k, v: _decode_attention(q, k, v, cache_k[i], cache_v[i], page_table, g_head2, lam,
                                          lam_init, n_dec, n_new),
        lambda z: _ssm_step(z, tables_step, state_ssm_re[i], state_ssm_im[i], n_dec, n_new),
        wts)

    lead = lambda t, *shape: t.reshape(1, *shape)
    return (y_p.reshape(batch, seq, D_MODEL), y_s.reshape(n_dec, n_new, D_MODEL),
            lead(k_p, batch, seq, N_HEADS, 2 * HEAD_DK), lead(v_p, batch, seq, N_HEADS, HEAD_DV),
            lead(sre_p, batch, SSM_GROUPS, SSM_STATE), lead(sim_p, batch, SSM_GROUPS, SSM_STATE),
            lead(k_s, n_dec, n_new, N_HEADS, 2 * HEAD_DK), lead(v_s, n_dec, n_new, N_HEADS, HEAD_DV),
            lead(sre_s, n_dec, SSM_GROUPS, SSM_STATE), lead(sim_s, n_dec, SSM_GROUPS, SSM_STATE))
```

```python
import functools
import math

import jax
import jax.numpy as jnp
from jax import lax
from jax.experimental import pallas as pl
from jax.experimental.pallas import tpu as pltpu

F32 = jnp.float32
BF16 = jnp.bfloat16

D_MODEL = 2048
PAGE_SIZE = 128
N_HEADS = 8
HEAD_DK = 64
HEAD_DV = 2 * HEAD_DK
QK_COLS = N_HEADS * 2 * HEAD_DK
ATTN_WIDTH = N_HEADS * HEAD_DV
ROPE_DIMS = HEAD_DK // 4
ROPE_THETA = 500000.0
NEG_INF = -1e30
SSM_WIDTH = D_MODEL // 2
SSM_GROUP = 16
SSM_GROUPS = SSM_WIDTH // SSM_GROUP
SSM_STATE = 64
PEER_HEADS = 8
PEER_KEYS = 128
PEER_EXPERTS = PEER_KEYS * PEER_KEYS
PEER_DKEY = 256
PEER_TOPK = 16
PLE_DIM = 256
RMS_EPS = 1e-6
V_END = 2 * QK_COLS + ATTN_WIDTH
U_END = V_END + SSM_WIDTH
IN_COLS = U_END + 2 * D_MODEL

LANES = 128
SUBLANES = 8
VMEM_LIMIT = 52 * 1024 * 1024
SSM_GB = LANES // SSM_GROUP
SSM_NB = SSM_GROUPS // SSM_GB
SSM_SW = SSM_GB * SSM_STATE
SSM_CHUNK = 512
SCAN_STEPS = (1, 2, 4)
DECODE_PAGES = 8
FLASH_BLOCK = 512
WBUILD_UNROLL = 8
EXPERT_TILE = 1024
QUERY_SCALE = HEAD_DK ** -0.5 * math.log2(math.e)
SAFE_SCORE_BOUND = 60.0


def _params(sem):
    return pltpu.CompilerParams(dimension_semantics=sem, vmem_limit_bytes=VMEM_LIMIT)


def _nt_dot(a, b):
    return lax.dot_general(a, b, (((1,), (1,)), ((), ())), preferred_element_type=F32)


def _in_proj_kernel(x_ref, g_ref, w_ref, o_ref, h_ref):
    @pl.when(pl.program_id(1) == 0)
    def _():
        x = x_ref[...]
        ms = jnp.mean(x * x, axis=-1, keepdims=True)
        h_ref[...] = (x * lax.rsqrt(ms + RMS_EPS) * g_ref[...]).astype(BF16)

    o_ref[...] = jnp.dot(h_ref[...], w_ref[...], preferred_element_type=F32)


def _in_proj(x, g, w_bf, tm, tn):
    n, d = x.shape
    cols = w_bf.shape[1]
    return pl.pallas_call(
        _in_proj_kernel,
        grid=(n // tm, cols // tn),
        in_specs=[
            pl.BlockSpec((tm, d), lambda i, j: (i, 0)),
            pl.BlockSpec((1, d), lambda i, j: (0, 0)),
            pl.BlockSpec((d, tn), lambda i, j: (0, j)),
        ],
        out_specs=pl.BlockSpec((tm, tn), lambda i, j: (i, j)),
        out_shape=jax.ShapeDtypeStruct((n, cols), F32),
        scratch_shapes=[pltpu.VMEM((tm, d), BF16)],
        compiler_params=_params(("parallel", "arbitrary")),
        name="in_proj",
    )(x, g, w_bf)


def _qkv_prep_kernel(zq_ref, zk_ref, zv_ref, gq_ref, gk_ref, bd_ref, c_ref, s1_ref, s2_ref,
                     q_ref, kf_ref, kb_ref, vb_ref):
    bd = bd_ref[...]
    cos, sin_lo, sin_hi = c_ref[...], s1_ref[...], s2_ref[...]

    def norm_rope(x, g):
        x2 = x * x
        hi = x2.astype(BF16)
        lo = (x2 - hi.astype(F32)).astype(BF16)
        ss = (jnp.dot(hi, bd, preferred_element_type=F32)
              + jnp.dot(lo, bd, preferred_element_type=F32))
        xn = x * lax.rsqrt(ss * (1.0 / HEAD_DK) + RMS_EPS) * g
        fwd = pltpu.roll(xn, LANES - ROPE_DIMS // 2, axis=1)
        bwd = pltpu.roll(xn, ROPE_DIMS // 2, axis=1)
        return xn * cos + fwd * sin_lo + bwd * sin_hi

    for c in range(QK_COLS // LANES):
        sl = slice(c * LANES, (c + 1) * LANES)
        q = norm_rope(zq_ref[:, sl], gq_ref[...])
        q_ref[:, sl] = (q * QUERY_SCALE).astype(BF16)
        k = norm_rope(zk_ref[:, sl], gk_ref[...])
        kf_ref[:, sl] = k
        kb_ref[:, sl] = k.astype(BF16)
    vb_ref[...] = zv_ref[...].astype(BF16)


def _qkv_prep(z, gq, gk, bd, cos, sin_lo, sin_hi, tm):
    n = z.shape[0]
    nt = cos.shape[0] // tm
    row = lambda c: pl.BlockSpec((tm, QK_COLS), lambda i: (i, c))
    vec = pl.BlockSpec((1, LANES), lambda i: (0, 0))
    tab = pl.BlockSpec((tm, LANES), lambda i: (i % nt, 0))
    out = pl.BlockSpec((tm, QK_COLS), lambda i: (i, 0))
    return pl.pallas_call(
        _qkv_prep_kernel,
        grid=(n // tm,),
        in_specs=[row(0), row(1), row(2), vec, vec,
                  pl.BlockSpec((LANES, LANES), lambda i: (0, 0)), tab, tab, tab],
        out_specs=[out, out, out, out],
        out_shape=[jax.ShapeDtypeStruct((n, QK_COLS), BF16),
                   jax.ShapeDtypeStruct((n, QK_COLS), F32),
                   jax.ShapeDtypeStruct((n, QK_COLS), BF16),
                   jax.ShapeDtypeStruct((n, ATTN_WIDTH), BF16)],
        compiler_params=_params(("parallel",)),
        name="qkv_prep",
    )(z, z, z, gq, gk, bd, cos, sin_lo, sin_hi)


def _rope_tables(pos):
    half = ROPE_DIMS // 2
    inv_freq = ROPE_THETA ** (-jnp.arange(half, dtype=F32) / half)
    ang = pos.astype(F32)[:, None] * inv_freq[None, :]
    cos, sin = jnp.cos(ang), jnp.sin(ang)
    ones = jnp.ones((pos.shape[0], HEAD_DK - ROPE_DIMS), F32)
    zeros8 = jnp.zeros_like(sin)
    zeros = jnp.zeros_like(ones)
    c = jnp.concatenate([cos, cos, ones], axis=1)
    s_lo = jnp.concatenate([-sin, zeros8, zeros], axis=1)
    s_hi = jnp.concatenate([zeros8, sin, zeros], axis=1)
    two = lambda a: jnp.concatenate([a, a], axis=1)
    return two(c), two(s_lo), two(s_hi)


def _head_finish(o, g, lam_init):
    ms = jnp.mean(o * o, axis=-1, keepdims=True)
    return o * lax.rsqrt(ms + RMS_EPS) * g * (1.0 - lam_init)


def _flash_kernel(lam_ref, q_ref, k_ref, v_ref, g_ref, o_ref, *, blk, lam_init):
    qi = pl.program_id(2)
    q = q_ref[0]
    lane = lax.broadcasted_iota(jnp.int32, q.shape, 1)
    zero = jnp.zeros_like(q)
    qq = jnp.concatenate([jnp.where(lane < HEAD_DK, q, zero),
                          jnp.where(lane >= HEAD_DK, q, zero)], axis=0)

    lam, bound = lam_ref[0], lam_ref[1]

    def scores(ki, masked):
        start = pl.multiple_of(ki * blk, blk)
        s = _nt_dot(qq, k_ref[0, pl.ds(start, blk), :])
        if masked:
            r = lax.broadcasted_iota(jnp.int32, s.shape, 0)
            c = lax.broadcasted_iota(jnp.int32, s.shape, 1)
            s = jnp.where(jnp.where(r >= blk, r - blk, r) >= c, s, NEG_INF)
        return s, v_ref[0, pl.ds(start, blk), :]

    def finish(l, acc):
        o = acc / l
        o = o[:blk] - lam * o[blk:]
        o_ref[0] = _head_finish(o, g_ref[...], lam_init).astype(BF16)

    zeros = (jnp.zeros((2 * blk, 1), F32), jnp.zeros((2 * blk, HEAD_DV), F32))

    @pl.when(bound <= SAFE_SCORE_BOUND)
    def _():
        def step(ki, carry, masked):
            l, acc = carry
            s, v = scores(ki, masked)
            p = jnp.exp2(s - bound)
            return (l + jnp.sum(p, axis=1, keepdims=True),
                    acc + jnp.dot(p.astype(BF16), v, preferred_element_type=F32))

        carry = lax.fori_loop(0, qi, lambda ki, c: step(ki, c, False), zeros)
        finish(*step(qi, carry, True))

    @pl.when(bound > SAFE_SCORE_BOUND)
    def _():
        def step(ki, carry, masked):
            m, l, acc = carry
            s, v = scores(ki, masked)
            m_new = jnp.maximum(m, jnp.max(s, axis=1, keepdims=True))
            alpha = jnp.exp2(m - m_new)
            p = jnp.exp2(s - m_new)
            return (m_new, alpha * l + jnp.sum(p, axis=1, keepdims=True),
                    alpha * acc + jnp.dot(p.astype(BF16), v, preferred_element_type=F32))

        init = (jnp.full((2 * blk, 1), NEG_INF, F32),) + zeros
        carry = lax.fori_loop(0, qi, lambda ki, c: step(ki, c, False), init)
        finish(*step(qi, carry, True)[1:])


def _flash_attention(q, k, v, g_head, lam, lam_init, batch, seq, blk):
    q3, k3, v3 = (t.reshape(batch, seq, QK_COLS) for t in (q, k, v))
    whole = pl.BlockSpec((1, seq, LANES), lambda b, h, i: (b, 0, h))
    out = pl.pallas_call(
        functools.partial(_flash_kernel, blk=blk, lam_init=lam_init),
        grid=(batch, N_HEADS, seq // blk),
        in_specs=[
            pl.BlockSpec(memory_space=pltpu.SMEM),
            pl.BlockSpec((1, blk, LANES), lambda b, h, i: (b, i, h)),
            whole, whole,
            pl.BlockSpec((1, LANES), lambda b, h, i: (0, 0)),
        ],
        out_specs=pl.BlockSpec((1, blk, LANES), lambda b, h, i: (b, i, h)),
        out_shape=jax.ShapeDtypeStruct((batch, seq, ATTN_WIDTH), BF16),
        compiler_params=_params(("parallel", "parallel", "arbitrary")),
        name="flash_attention",
    )(lam, q3, k3, v3, g_head)
    return out.reshape(batch * seq, ATTN_WIDTH)


def _decode_kernel(pt_ref, lam_ref, qt_ref, bias_ref, biasn_ref, kn_ref, vn_ref, g_ref, *rest,
                   n_rows, lam_init):
    k_refs, v_refs = rest[:DECODE_PAGES], rest[DECODE_PAGES:2 * DECODE_PAGES]
    o_ref, m_ref, l_ref, acc_ref, s_ref = rest[2 * DECODE_PAGES:]
    step = pl.program_id(1)
    is_last = step == pl.num_programs(1) - 1
    page_rows = PAGE_SIZE * N_HEADS

    @pl.when(step == 0)
    def _():
        m_ref[...] = jnp.full_like(m_ref, NEG_INF)
        l_ref[...] = jnp.zeros_like(l_ref)
        acc_ref[...] = jnp.zeros_like(acc_ref)

    eye = (lax.broadcasted_iota(jnp.int32, (LANES, LANES), 0)
           == lax.broadcasted_iota(jnp.int32, (LANES, LANES), 1))

    def to_col(row):
        return jnp.sum(jnp.where(eye, jnp.broadcast_to(row, (LANES, LANES)), 0.0), axis=1, keepdims=True)

    def contract_rows(p, v):
        return lax.dot_general(p.astype(BF16), v, (((0,), (0,)), ((), ())), preferred_element_type=F32)

    qt = qt_ref[0]
    bias = bias_ref[...]
    m_prev = m_ref[...]
    m_new = m_prev
    for r in range(DECODE_PAGES):
        k2d = k_refs[r][...].reshape(page_rows, LANES).astype(BF16)
        s = jnp.dot(k2d, qt, preferred_element_type=F32) + bias
        s_ref[r] = s
        m_new = jnp.maximum(m_new, jnp.max(s, axis=0, keepdims=True))
    s_new = (jnp.dot(kn_ref[0], qt, preferred_element_type=F32) + biasn_ref[...]
             + jnp.where(is_last, 0.0, NEG_INF))
    m_new = jnp.maximum(m_new, jnp.max(s_new, axis=0, keepdims=True))

    alpha = jnp.exp2(m_prev - m_new)
    p_new = jnp.exp2(s_new - m_new)
    l = alpha * l_ref[...] + jnp.sum(p_new, axis=0, keepdims=True)
    pv = contract_rows(p_new, vn_ref[0])
    for r in range(DECODE_PAGES):
        p = jnp.exp2(s_ref[r] - m_new)
        l = l + jnp.sum(p, axis=0, keepdims=True)
        pv = pv + contract_rows(p, v_refs[r][...].reshape(page_rows, LANES).astype(BF16))
    acc = to_col(alpha) * acc_ref[...] + pv
    acc_ref[...] = acc
    m_ref[...] = m_new
    l_ref[...] = l

    @pl.when(is_last)
    def _():
        o = acc / to_col(l)
        o = o[:n_rows] - lam_ref[0] * o[n_rows:2 * n_rows]
        o_ref[0] = _head_finish(o, g_ref[...], lam_init).astype(BF16)


def _decode_attention(q, k_new, v_new, cache_k, cache_v, page_table, g_head, lam, lam_init, n_dec, n_new):
    n_pages = page_table.shape[1]
    n_rows = N_HEADS * n_new
    q5 = q.reshape(n_dec, n_new, N_HEADS, 2, HEAD_DK)
    qt = jnp.einsum("bthmd,mM->bMdmht", q5, jnp.eye(2, dtype=BF16)).reshape(n_dec, LANES, 2 * n_rows)
    qt = jnp.pad(qt, ((0, 0), (0, 0), (0, LANES - 2 * n_rows)))
    col = jnp.arange(LANES)
    col_ok, col_head, col_t = col < 2 * n_rows, (col % n_rows) // n_new, col % n_new
    row = jnp.arange(PAGE_SIZE * N_HEADS)
    bias = jnp.where(col_ok[None] & (row[:, None] % N_HEADS == col_head[None]), 0.0, NEG_INF).astype(F32)
    row_n = jnp.arange(n_rows)
    bias_new = jnp.where(col_ok[None] & (row_n[:, None] % N_HEADS == col_head[None])
                         & (row_n[:, None] // N_HEADS <= col_t[None]), 0.0, NEG_INF).astype(F32)
    kn = k_new.reshape(n_dec, n_rows, LANES)
    vn = v_new.reshape(n_dec, n_rows, LANES)

    def page_spec(r):
        return pl.BlockSpec((None, PAGE_SIZE, N_HEADS, LANES),
                            lambda b, s, pt: (pt[b * n_pages + s * DECODE_PAGES + r], 0, 0, 0))

    const = lambda shape: pl.BlockSpec(shape, lambda b, s, pt: (0,) * len(shape))
    per_b = lambda shape: pl.BlockSpec(shape, lambda b, s, pt: (b,) + (0,) * (len(shape) - 1))
    pages = [page_spec(r) for r in range(DECODE_PAGES)]
    grid_spec = pltpu.PrefetchScalarGridSpec(
        num_scalar_prefetch=1,
        grid=(n_dec, n_pages // DECODE_PAGES),
        in_specs=[pl.BlockSpec(memory_space=pltpu.SMEM), per_b((1, LANES, LANES)),
                  const((PAGE_SIZE * N_HEADS, LANES)), const((n_rows, LANES)),
                  per_b((1, n_rows, LANES)), per_b((1, n_rows, LANES)), const((1, LANES))]
                 + pages + pages,
        out_specs=per_b((1, n_rows, HEAD_DV)),
        scratch_shapes=[pltpu.VMEM((1, LANES), F32), pltpu.VMEM((1, LANES), F32),
                        pltpu.VMEM((LANES, HEAD_DV), F32),
                        pltpu.VMEM((DECODE_PAGES, PAGE_SIZE * N_HEADS, LANES), F32)],
    )
    out = pl.pallas_call(
        functools.partial(_decode_kernel, n_rows=n_rows, lam_init=lam_init),
        grid_spec=grid_spec,
        out_shape=jax.ShapeDtypeStruct((n_dec, n_rows, HEAD_DV), BF16),
        compiler_params=_params(("parallel", "arbitrary")),
        name="decode_attention",
    )(page_table.reshape(-1), lam, qt, bias, bias_new, kn, vn, g_head,
      *([cache_k] * DECODE_PAGES), *([cache_v] * DECODE_PAGES))
    out = out.reshape(n_dec, N_HEADS, n_new, HEAD_DV).transpose(0, 2, 1, 3)
    return out.reshape(n_dec * n_new, ATTN_WIDTH)


def _ssm_tables(a_re, a_im, log_dt, b_re, b_im, c_re, c_im, d_skip):
    dt = jnp.exp(log_dt)[:, None]
    mag = jnp.exp(a_re * dt)
    ab_re, ab_im = mag * jnp.cos(a_im * dt), mag * jnp.sin(a_im * dt)
    den = a_re * a_re + a_im * a_im
    f_re = ((ab_re - 1.0) * a_re + ab_im * a_im) / den
    f_im = (ab_im * a_re - (ab_re - 1.0) * a_im) / den
    bb_re = f_re[..., None] * b_re - f_im[..., None] * b_im
    bb_im = f_re[..., None] * b_im + f_im[..., None] * b_re
    eye = jnp.eye(SSM_GB, dtype=F32)

    def b_blocks(bb):
        bb = bb.reshape(SSM_NB, SSM_GB, SSM_STATE, SSM_GROUP)
        return jnp.einsum("agpc,gh->agchp", bb, eye).reshape(SSM_NB, LANES, SSM_SW)

    def c_blocks(cc):
        cc = cc.reshape(SSM_NB, SSM_GB, SSM_GROUP, SSM_STATE)
        return jnp.einsum("agcp,gh->agphc", cc, eye).reshape(SSM_NB, SSM_SW, LANES)

    bbd = jnp.concatenate([b_blocks(bb_re), b_blocks(bb_im)], axis=2).astype(BF16)
    cbd = jnp.concatenate([c_blocks(c_re), c_blocks(-c_im)], axis=1).astype(BF16)
    d = d_skip.reshape(SSM_NB, 1, LANES)
    pr, pi = ab_re[None], ab_im[None]
    while pr.shape[0] < SUBLANES:
        tr, ti = pr[-1:], pi[-1:]
        pr, pi = (jnp.concatenate([pr, pr * tr - pi * ti], axis=0),
                  jnp.concatenate([pi, pr * ti + pi * tr], axis=0))
    lay = lambda t: t.reshape(t.shape[0], SSM_NB, SSM_SW).transpose(1, 0, 2)
    apow = jnp.concatenate([lay(pr), lay(pi)], axis=2)
    rows = jnp.arange(SUBLANES)[None, :, None]
    steps = [jnp.where(rows >= dd, apow[:, dd - 1:dd, :], 0.0) for dd in SCAN_STEPS]
    amask = jnp.stack(steps, axis=1)
    return bbd, cbd, d, apow, amask


def _ssm_prompt_kernel(u_ref, b_ref, c_ref, d_ref, ap_ref, am_ref, y_ref, sre_ref, sim_ref, h_ref):
    chunk = pl.program_id(2)

    @pl.when(chunk == 0)
    def _():
        h_ref[...] = jnp.zeros_like(h_ref)

    u = u_ref[0]
    length = u.shape[0]
    bu = jnp.dot(u.astype(BF16), b_ref[0], preferred_element_type=F32)
    pr, pi = ap_ref[0, :, :SSM_SW], ap_ref[0, :, SSM_SW:]
    cr, ci = h_ref[0:1, :SSM_SW], h_ref[0:1, SSM_SW:]
    out_re, out_im = [], []
    for j in range(length // SUBLANES):
        rows = slice(j * SUBLANES, (j + 1) * SUBLANES)
        re, im = bu[rows, :SSM_SW], bu[rows, SSM_SW:]
        for si, dd in enumerate(SCAN_STEPS):
            mr, mi = am_ref[0, si, :, :SSM_SW], am_ref[0, si, :, SSM_SW:]
            sr, s_i = pltpu.roll(re, dd, axis=0), pltpu.roll(im, dd, axis=0)
            re, im = re + mr * sr - mi * s_i, im + mr * s_i + mi * sr
        re, im = re + pr * cr - pi * ci, im + pr * ci + pi * cr
        cr, ci = re[SUBLANES - 1:SUBLANES], im[SUBLANES - 1:SUBLANES]
        out_re.append(re)
        out_im.append(im)
    h_ref[0:1, :SSM_SW] = cr
    h_ref[0:1, SSM_SW:] = ci
    hcat = jnp.concatenate([jnp.concatenate(out_re, axis=0), jnp.concatenate(out_im, axis=0)],
                           axis=1).astype(BF16)
    y = jnp.dot(hcat, c_ref[0], preferred_element_type=F32) + d_ref[0] * u
    y_ref[0] = jax.nn.gelu(y).astype(BF16)

    @pl.when(chunk == pl.num_programs(2) - 1)
    def _():
        sre_ref[0, 0] = cr
        sim_ref[0, 0] = ci


def _ssm_prompt(z, tables, batch, seq):
    bbd, cbd, d, apow, amask = tables
    length = SSM_CHUNK
    z3 = z.reshape(batch, seq, IN_COLS)
    u_blk = V_END // LANES
    y, sre, sim = pl.pallas_call(
        _ssm_prompt_kernel,
        grid=(batch, SSM_NB, seq // length),
        in_specs=[
            pl.BlockSpec((1, length, LANES), lambda b, g, c: (b, c, u_blk + g)),
            pl.BlockSpec((1, LANES, 2 * SSM_SW), lambda b, g, c: (g, 0, 0)),
            pl.BlockSpec((1, 2 * SSM_SW, LANES), lambda b, g, c: (g, 0, 0)),
            pl.BlockSpec((1, 1, LANES), lambda b, g, c: (g, 0, 0)),
            pl.BlockSpec((1, SUBLANES, 2 * SSM_SW), lambda b, g, c: (g, 0, 0)),
            pl.BlockSpec((1, len(SCAN_STEPS), SUBLANES, 2 * SSM_SW), lambda b, g, c: (g, 0, 0, 0)),
        ],
        out_specs=[
            pl.BlockSpec((1, length, LANES), lambda b, g, c: (b, c, g)),
            pl.BlockSpec((1, 1, 1, SSM_SW), lambda b, g, c: (b, g, 0, 0)),
            pl.BlockSpec((1, 1, 1, SSM_SW), lambda b, g, c: (b, g, 0, 0)),
        ],
        out_shape=[jax.ShapeDtypeStruct((batch, seq, SSM_WIDTH), BF16),
                   jax.ShapeDtypeStruct((batch, SSM_NB, 1, SSM_SW), F32),
                   jax.ShapeDtypeStruct((batch, SSM_NB, 1, SSM_SW), F32)],
        scratch_shapes=[pltpu.VMEM((SUBLANES, 2 * SSM_SW), F32)],
        compiler_params=_params(("parallel", "parallel", "arbitrary")),
        name="ssm_prompt",
    )(z3, bbd, cbd, d, apow, amask)
    state = lambda s: s.reshape(batch, SSM_GROUPS, SSM_STATE)
    return y.reshape(batch * seq, SSM_WIDTH), state(sre), state(sim)


def _ssm_step_kernel(u_ref, b_ref, c_ref, d_ref, ap_ref, hre_ref, him_ref, y_ref, sre_ref, sim_ref):
    ar, ai = ap_ref[0, 0:1, :SSM_SW], ap_ref[0, 0:1, SSM_SW:]
    re, im = hre_ref[...], him_ref[...]
    bmat = b_ref[0]
    for t in range(u_ref.shape[0]):
        u = u_ref[t]
        u_hi = u.astype(BF16)
        u_lo = (u - u_hi.astype(F32)).astype(BF16)
        bu = (jnp.dot(u_hi, bmat, preferred_element_type=F32)
              + jnp.dot(u_lo, bmat, preferred_element_type=F32))
        re, im = (ar * re - ai * im + bu[:, :SSM_SW], ar * im + ai * re + bu[:, SSM_SW:])
        hcat = jnp.concatenate([re, im], axis=1).astype(BF16)
        y = jnp.dot(hcat, c_ref[0], preferred_element_type=F32) + d_ref[0] * u
        y_ref[t] = jax.nn.gelu(y).astype(BF16)
    sre_ref[...] = re
    sim_ref[...] = im


def _ssm_step(z, tables, h_re, h_im, batch, steps):
    bbd, cbd, d, apow, _ = tables
    u = z[:, V_END:U_END].reshape(batch, steps, SSM_WIDTH).transpose(1, 0, 2)
    flat = lambda s: s.reshape(batch, SSM_GROUPS * SSM_STATE)
    y, sre, sim = pl.pallas_call(
        _ssm_step_kernel,
        grid=(SSM_NB,),
        in_specs=[
            pl.BlockSpec((steps, batch, LANES), lambda g: (0, 0, g)),
            pl.BlockSpec((1, LANES, 2 * SSM_SW), lambda g: (g, 0, 0)),
            pl.BlockSpec((1, 2 * SSM_SW, LANES), lambda g: (g, 0, 0)),
            pl.BlockSpec((1, 1, LANES), lambda g: (g, 0, 0)),
            pl.BlockSpec((1, SUBLANES, 2 * SSM_SW), lambda g: (g, 0, 0)),
            pl.BlockSpec((batch, SSM_SW), lambda g: (0, g)),
            pl.BlockSpec((batch, SSM_SW), lambda g: (0, g)),
        ],
        out_specs=[
            pl.BlockSpec((steps, batch, LANES), lambda g: (0, 0, g)),
            pl.BlockSpec((batch, SSM_SW), lambda g: (0, g)),
            pl.BlockSpec((batch, SSM_SW), lambda g: (0, g)),
        ],
        out_shape=[jax.ShapeDtypeStruct((steps, batch, SSM_WIDTH), BF16),
                   jax.ShapeDtypeStruct((batch, SSM_GROUPS * SSM_STATE), F32),
                   jax.ShapeDtypeStruct((batch, SSM_GROUPS * SSM_STATE), F32)],
        compiler_params=_params(("parallel",)),
        name="ssm_step",
    )(u, bbd, cbd, d, apow, flat(h_re), flat(h_im))
    y = y.transpose(1, 0, 2).reshape(batch * steps, SSM_WIDTH)
    state = lambda s: s.reshape(batch, SSM_GROUPS, SSM_STATE)
    return y, state(sre), state(sim)


def _merge_kernel(att_ref, ys_ref, za_ref, zs_ref, wup_ref, wga_ref, wgb_ref, o_ref):
    ys = ys_ref[...]
    branch_a = jnp.dot(att_ref[...], wup_ref[...], preferred_element_type=F32)
    glu_a = jnp.dot(ys, wga_ref[...], preferred_element_type=F32)
    glu_b = jnp.dot(ys, wgb_ref[...], preferred_element_type=F32)
    branch_s = glu_a * jax.nn.sigmoid(glu_b)
    merged = jax.nn.sigmoid(za_ref[...]) * branch_a + jax.nn.sigmoid(zs_ref[...]) * branch_s
    o_ref[...] = merged.astype(BF16)


def _merge(att, ys, z, w_up, w_glu, tm, tn):
    n = att.shape[0]
    nj = D_MODEL // tn
    act = pl.BlockSpec((tm, ATTN_WIDTH), lambda j, i: (i, 0))
    gate = lambda off: pl.BlockSpec((tm, tn), lambda j, i: (i, off + j))
    wcol = lambda off: pl.BlockSpec((ATTN_WIDTH, tn), lambda j, i: (0, off + j))
    return pl.pallas_call(
        _merge_kernel,
        grid=(nj, n // tm),
        in_specs=[act, act, gate(U_END // tn), gate((U_END + D_MODEL) // tn),
                  wcol(0), wcol(0), wcol(nj)],
        out_specs=pl.BlockSpec((tm, tn), lambda j, i: (i, j)),
        out_shape=jax.ShapeDtypeStruct((n, D_MODEL), BF16),
        compiler_params=_params(("parallel", "parallel")),
        name="merge",
    )(att, ys, z, z, w_up, w_glu, w_glu)


def _out_proj_kernel(x_ref, m_ref, w_ref, g_ref, x1_ref, hn_ref):
    x1 = x_ref[...] + jnp.dot(m_ref[...], w_ref[...], preferred_element_type=F32)
    x1_ref[...] = x1
    ms = jnp.mean(x1 * x1, axis=-1, keepdims=True)
    hn_ref[...] = (x1 * lax.rsqrt(ms + RMS_EPS) * g_ref[...]).astype(BF16)


def _out_proj(x, merged, w_out, g_ffn, tm):
    n = x.shape[0]
    row = pl.BlockSpec((tm, D_MODEL), lambda i: (i, 0))
    return pl.pallas_call(
        _out_proj_kernel,
        grid=(n // tm,),
        in_specs=[row, row, pl.BlockSpec((D_MODEL, D_MODEL), lambda i: (0, 0)),
                  pl.BlockSpec((1, D_MODEL), lambda i: (0, 0))],
        out_specs=[row, row],
        out_shape=[jax.ShapeDtypeStruct((n, D_MODEL), F32),
                   jax.ShapeDtypeStruct((n, D_MODEL), BF16)],
        compiler_params=_params(("parallel",)),
        name="out_proj",
    )(x, merged, w_out, g_ffn)


def _top_rows(s, k):
    n = s.shape[0]
    row = lax.broadcasted_iota(jnp.int32, s.shape, 0)
    vals, idxs = [], []
    for _ in range(k):
        best = jnp.max(s, axis=0, keepdims=True)
        pick = jnp.min(jnp.where(s == best, row, n), axis=0, keepdims=True)
        vals.append(best)
        idxs.append(pick)
        s = jnp.where(row == pick, -jnp.inf, s)
    return jnp.concatenate(vals, axis=0), jnp.concatenate(idxs, axis=0)


def _router_kernel(h_ref, wq_ref, sk_ref, a_ref, b_ref, g_ref):
    tm = h_ref.shape[0]
    k = PEER_TOPK
    q = jnp.dot(h_ref[...], wq_ref[...], preferred_element_type=F32).astype(BF16)
    half = PEER_DKEY // 2

    n_rows = k + 8 * SUBLANES
    r = lax.broadcasted_iota(jnp.int32, (n_rows, tm), 0)
    p_mid = ((r - k) >> 3) + 1
    q_mid = (r - k) & 7
    valid = (r < k) | (r >= k + 7 * SUBLANES) | ((p_mid + 1) * (q_mid + 1) <= k)
    r16 = lax.broadcasted_iota(jnp.int32, (k, tm), 0)

    def cand_rows(first, second):
        parts = [first[0:1] + second[0:k]]
        parts += [first[p:p + 1] + second[0:SUBLANES] for p in range(1, 8)]
        parts.append(first[8:16] + second[0:1])
        return jnp.concatenate(parts, axis=0)

    for h in range(PEER_HEADS):
        tops = []
        for m in range(2):
            c = 2 * h + m
            s = _nt_dot(sk_ref[m], q[:, c * half:(c + 1) * half])
            tops.append(_top_rows(s, k))
        (v1, i1), (v2, i2) = tops
        cand = jnp.where(valid, cand_rows(v1, v2), -jnp.inf)
        fv, frow = _top_rows(cand, k)
        tail = k + 7 * SUBLANES
        p_pos = jnp.where(frow < k, 0, jnp.where(frow >= tail, frow - tail + SUBLANES, ((frow - k) >> 3) + 1))
        q_pos = jnp.where(frow < k, frow, jnp.where(frow >= tail, 0, (frow - k) & 7))
        sel_a, sel_b = [], []
        for j in range(k):
            sel_a.append(jnp.max(jnp.where(r16 == p_pos[j:j + 1], i1, -1), axis=0, keepdims=True))
            sel_b.append(jnp.max(jnp.where(r16 == q_pos[j:j + 1], i2, -1), axis=0, keepdims=True))
        e = jnp.exp(fv - fv[0:1])
        gate = e / jnp.sum(e, axis=0, keepdims=True)
        a_ref[h * k:(h + 1) * k, :] = jnp.concatenate(sel_a, axis=0)
        b_ref[h * k:(h + 1) * k, :] = jnp.concatenate(sel_b, axis=0)
        g_ref[h * k:(h + 1) * k, :] = gate


def _router(hn, w_query, sub_keys, tm):
    n = hn.shape[0]
    slots = PEER_HEADS * PEER_TOPK
    out = pl.BlockSpec((slots, tm), lambda i: (0, i))
    return pl.pallas_call(
        _router_kernel,
        grid=(n // tm,),
        in_specs=[pl.BlockSpec((tm, D_MODEL), lambda i: (i, 0)),
                  pl.BlockSpec((D_MODEL, PEER_HEADS * PEER_DKEY), lambda i: (0, 0)),
                  pl.BlockSpec((2, PEER_KEYS, PEER_DKEY // 2), lambda i: (0, 0, 0))],
        out_specs=[out, out, out],
        out_shape=[jax.ShapeDtypeStruct((slots, n), jnp.int32),
                   jax.ShapeDtypeStruct((slots, n), jnp.int32),
                   jax.ShapeDtypeStruct((slots, n), F32)],
        compiler_params=_params(("parallel",)),
        name="router",
    )(hn, w_query, sub_keys)


def _wbuild_kernel(a_ref, b_ref, g_ref, w_ref):
    tm = a_ref.shape[0]
    idx = lax.broadcasted_iota(jnp.int32, (PEER_KEYS, a_ref.shape[1]), 0)
    grouped = (PEER_KEYS // SUBLANES, SUBLANES, PEER_KEYS)
    sub = lax.broadcasted_iota(jnp.int32, grouped, 1)

    def body(i, carry):
        base = pl.multiple_of(i * WBUILD_UNROLL, WBUILD_UNROLL)
        a_rows = a_ref[pl.ds(base, WBUILD_UNROLL), :]
        b_rows = b_ref[pl.ds(base, WBUILD_UNROLL), :]
        g_rows = g_ref[pl.ds(base, WBUILD_UNROLL), :]
        planes = []
        for t in range(WBUILD_UNROLL):
            first = jnp.where(a_rows[t:t + 1] == idx, g_rows[t:t + 1], 0.0).astype(BF16)
            second = jnp.where(b_rows[t:t + 1] == idx, 1.0, 0.0).astype(BF16)
            planes.append(_nt_dot(first, second).reshape(grouped))
        for d in (4, 2, 1):
            upper = (sub & d) != 0
            for t in range(WBUILD_UNROLL):
                if t & d == 0:
                    lo, hi = planes[t], planes[t + d]
                    planes[t] = jnp.where(upper, pltpu.roll(hi, d, axis=1), lo)
                    planes[t + d] = jnp.where(upper, hi, pltpu.roll(lo, SUBLANES - d, axis=1))
        for r in range(WBUILD_UNROLL):
            w_ref[i, :, r] = planes[r]
        return carry

    lax.fori_loop(0, tm // WBUILD_UNROLL, body, 0)


def _wbuild(a, b, g, tm):
    n, slots = a.shape
    row = pl.BlockSpec((tm, slots), lambda i: (i, 0))
    key_hi = PEER_KEYS // SUBLANES
    w = pl.pallas_call(
        _wbuild_kernel,
        grid=(n // tm,),
        in_specs=[row, row, row],
        out_specs=pl.BlockSpec((tm // WBUILD_UNROLL, key_hi, SUBLANES, WBUILD_UNROLL, PEER_KEYS),
                               lambda i: (i, 0, 0, 0, 0)),
        out_shape=jax.ShapeDtypeStruct((n // WBUILD_UNROLL, key_hi, SUBLANES, WBUILD_UNROLL, PEER_KEYS), F32),
        compiler_params=_params(("parallel",)),
        name="wbuild",
    )(a, b, g)
    return w.reshape(n // WBUILD_UNROLL, PEER_KEYS, WBUILD_UNROLL, PEER_KEYS)


def _experts_kernel(h_ref, ut_ref, v_ref, w_ref, o_ref):
    @pl.when(pl.program_id(1) == 0)
    def _():
        o_ref[...] = jnp.zeros_like(o_ref)

    tm = h_ref.shape[0]
    s = jnp.dot(h_ref[...], ut_ref[...], preferred_element_type=F32)
    w = jnp.concatenate([w_ref[:, r].reshape(tm, PEER_KEYS) for r in range(w_ref.shape[1])], axis=1)
    c = (jax.nn.gelu(s) * w).astype(BF16)
    o_ref[...] += jnp.dot(c, v_ref[...], preferred_element_type=F32)


def _experts(hn, u_t, v, w, tm):
    n = hn.shape[0]
    keys_per_tile = EXPERT_TILE // PEER_KEYS
    return pl.pallas_call(
        _experts_kernel,
        grid=(n // tm, PEER_EXPERTS // EXPERT_TILE),
        in_specs=[pl.BlockSpec((tm, D_MODEL), lambda i, j: (i, 0)),
                  pl.BlockSpec((D_MODEL, EXPERT_TILE), lambda i, j: (0, j)),
                  pl.BlockSpec((EXPERT_TILE, D_MODEL), lambda i, j: (j, 0)),
                  pl.BlockSpec((tm // WBUILD_UNROLL, keys_per_tile, WBUILD_UNROLL, PEER_KEYS),
                               lambda i, j: (i, j, 0, 0))],
        out_specs=pl.BlockSpec((tm, D_MODEL), lambda i, j: (i, 0)),
        out_shape=jax.ShapeDtypeStruct((n, D_MODEL), F32),
        compiler_params=_params(("parallel", "arbitrary")),
        name="experts",
    )(hn, u_t, v, w)


def _ple_kernel(x1_ref, peer_ref, p_ref, wp_ref, wg_ref, g_ref, o_ref):
    x2 = x1_ref[...] + peer_ref[...]
    ms = jnp.mean(x2 * x2, axis=-1, keepdims=True)
    hn = (x2 * lax.rsqrt(ms + RMS_EPS) * g_ref[...]).astype(BF16)
    emb = jnp.dot(p_ref[...].astype(BF16), wp_ref[...], preferred_element_type=F32)
    gate = jax.nn.sigmoid(jnp.dot(hn, wg_ref[...], preferred_element_type=F32))
    o_ref[...] = x2 + emb * gate


def _ple(x1, peer, p, w_ple, w_gate, g_ple, tm):
    n = x1.shape[0]
    row = pl.BlockSpec((tm, D_MODEL), lambda i: (i, 0))
    return pl.pallas_call(
        _ple_kernel,
        grid=(n // tm,),
        in_specs=[row, row, pl.BlockSpec((tm, PLE_DIM), lambda i: (i, 0)),
                  pl.BlockSpec((PLE_DIM, D_MODEL), lambda i: (0, 0)),
                  pl.BlockSpec((D_MODEL, D_MODEL), lambda i: (0, 0)),
                  pl.BlockSpec((1, D_MODEL), lambda i: (0, 0))],
        out_specs=row,
        out_shape=jax.ShapeDtypeStruct((n, D_MODEL), F32),
        compiler_params=_params(("parallel",)),
        name="ple",
    )(x1, peer, p, w_ple, w_gate, g_ple)


def _tile(n, pref):
    return pref if n % pref == 0 else n


def _layer(x, p, pos_rows, attention, ssm, wts):
    n = x.shape[0]
    z = _in_proj(x, wts["g_mix"], wts["w_in"], _tile(n, 1024), 1024)
    cos, sin_lo, sin_hi = _rope_tables(pos_rows)
    q, k_f32, k_bf, v_bf = _qkv_prep(z, wts["g_q"], wts["g_k"], wts["bd"], cos, sin_lo, sin_hi,
                                     _tile(cos.shape[0], 512))
    att = attention(q, k_bf, v_bf)
    ys, s_re, s_im = ssm(z)
    merged = _merge(att, ys, z, wts["w_attn_up"], wts["w_glu"], _tile(n, 512), 1024)
    x1, hn = _out_proj(x, merged, wts["w_out"], wts["g_ffn"], _tile(n, 512))
    a_t, b_t, g_t = _router(hn, wts["peer_w_query"], wts["peer_sub_keys"], 128)
    w = _wbuild(a_t.T, b_t.T, g_t.T, _tile(n, 128))
    peer = _experts(hn, wts["peer_u_t"], wts["peer_v"], w, _tile(n, 512))
    y = _ple(x1, peer, p, wts["w_ple"], wts["w_ple_gate"], wts["g_ple"], _tile(n, 512))
    k_new = k_f32.reshape(n, N_HEADS, 2 * HEAD_DK)
    v_new = z[:, 2 * QK_COLS:V_END].reshape(n, N_HEADS, HEAD_DV)
    return y, k_new, v_new, s_re, s_im


def kernel(x_prompt, x_sample, cache_k, cache_v, state_ssm_re, state_ssm_im, page_table, p_prompt, p_sample, g_mix, w_in, g_q, g_k, lambda_q, lambda_k, g_head, w_attn_up, ssm_a_re, ssm_a_im, ssm_log_dt, ssm_b_re, ssm_b_im, ssm_c_re, ssm_c_im, ssm_d, w_glu, w_out, g_ffn, peer_w_query, peer_sub_keys, peer_u, peer_v, g_ple, w_ple, w_ple_gate):
    depth = w_in.shape[0]
    assert depth == 1
    batch, seq, _ = x_prompt.shape
    n_dec, n_new, _ = x_sample.shape
    past_len = page_table.shape[1] * PAGE_SIZE
    i = 0
    lam_init = 0.8 - 0.6 * math.exp(-0.3 * i)

    row = lambda t: t.reshape(1, -1)
    bd = jnp.kron(jnp.eye(2, dtype=F32), jnp.ones((HEAD_DK, HEAD_DK), F32)).astype(BF16)
    wts = {
        "g_mix": row(g_mix[i]), "w_in": w_in[i].astype(BF16),
        "g_q": row(g_q[i]), "g_k": row(g_k[i]), "bd": bd,
        "w_attn_up": w_attn_up[i].astype(BF16), "w_glu": w_glu[i].astype(BF16),
        "w_out": w_out[i].astype(BF16), "g_ffn": row(g_ffn[i]),
        "peer_w_query": peer_w_query[i].astype(BF16), "peer_sub_keys": peer_sub_keys[i].astype(BF16),
        "peer_u_t": peer_u[i].T.astype(BF16), "peer_v": peer_v[i].astype(BF16),
        "g_ple": row(g_ple[i]), "w_ple": w_ple[i].astype(BF16), "w_ple_gate": w_ple_gate[i].astype(BF16),
    }
    g_head2 = row(g_head[i])
    lq, lk = lambda_q[i].astype(F32), lambda_k[i].astype(F32)
    lam = jnp.exp(jnp.sum(lq[0] * lk[0])) - jnp.exp(jnp.sum(lq[1] * lk[1])) + lam_init
    score_bound = (1.05 * QUERY_SCALE * HEAD_DK) * jnp.max(jnp.abs(g_q[i])) * jnp.max(jnp.abs(g_k[i]))
    lam = jnp.stack([lam, score_bound]).astype(F32)
    ssm_args = (ssm_a_re[i], ssm_a_im[i], ssm_log_dt[i], ssm_b_re[i], ssm_b_im[i],
                ssm_c_re[i], ssm_c_im[i], ssm_d[i])
    tables_prompt = tables_step = _ssm_tables(*ssm_args)

    pos_prompt = jnp.arange(seq, dtype=jnp.int32)
    y_p, k_p, v_p, sre_p, sim_p = _layer(
        x_prompt.reshape(batch * seq, D_MODEL), p_prompt[i].reshape(batch * seq, PLE_DIM), pos_prompt,
        lambda q, k, v: _flash_attention(q, k, v, g_head2, lam, lam_init, batch, seq, FLASH_BLOCK),
        lambda z: _ssm_prompt(z, tables_prompt, batch, seq),
        wts)

    pos_sample = jnp.tile(past_len + jnp.arange(n_new, dtype=jnp.int32), n_dec)
    y_s, k_s, v_s, sre_s, sim_s = _layer(
        x_sample.reshape(n_dec * n_new, D_MODEL), p_sample[i].reshape(n_dec * n_new, PLE_DIM), pos_sample,
        lambda q, k, v: _decode_attention(q, k, v, cache_k[i], cache_v[i], page_table, g_head2, lam,
                                          lam_init, n_dec, n_new),
        lambda z: _ssm_step(z, tables_step, state_ssm_re[i], state_ssm_im[i], n_dec, n_new),
        wts)

    lead = lambda t, *shape: t.reshape(1, *shape)
    return (y_p.reshape(batch, seq, D_MODEL), y_s.reshape(n_dec, n_new, D_MODEL),
            lead(k_p, batch, seq, N_HEADS, 2 * HEAD_DK), lead(v_p, batch, seq, N_HEADS, HEAD_DV),
            lead(sre_p, batch, SSM_GROUPS, SSM_STATE), lead(sim_p, batch, SSM_GROUPS, SSM_STATE),
            lead(k_s, n_dec, n_new, N_HEADS, 2 * HEAD_DK), lead(v_s, n_dec, n_new, N_HEADS, HEAD_DV),
            lead(sre_s, n_dec, SSM_GROUPS, SSM_STATE), lead(sim_s, n_dec, SSM_GROUPS, SSM_STATE))
```

```python
import functools
import math

import jax
import jax.numpy as jnp
from jax import lax
from jax.experimental import pallas as pl
from jax.experimental.pallas import tpu as pltpu

F32 = jnp.float32
BF16 = jnp.bfloat16

D_MODEL = 2048
PAGE_SIZE = 128
N_HEADS = 8
HEAD_DK = 64
HEAD_DV = 2 * HEAD_DK
QK_COLS = N_HEADS * 2 * HEAD_DK
ATTN_WIDTH = N_HEADS * HEAD_DV
ROPE_DIMS = HEAD_DK // 4
ROPE_THETA = 500000.0
NEG_INF = -1e30
SSM_WIDTH = D_MODEL // 2
SSM_GROUP = 16
SSM_GROUPS = SSM_WIDTH // SSM_GROUP
SSM_STATE = 64
PEER_HEADS = 8
PEER_KEYS = 128
PEER_EXPERTS = PEER_KEYS * PEER_KEYS
PEER_DKEY = 256
PEER_TOPK = 16
PLE_DIM = 256
RMS_EPS = 1e-6
V_END = 2 * QK_COLS + ATTN_WIDTH
U_END = V_END + SSM_WIDTH
IN_COLS = U_END + 2 * D_MODEL

LANES = 128
SUBLANES = 8
VMEM_LIMIT = 52 * 1024 * 1024
SSM_GB = LANES // SSM_GROUP
SSM_NB = SSM_GROUPS // SSM_GB
SSM_SW = SSM_GB * SSM_STATE
SSM_CHUNK = 512
SCAN_STEPS = (1, 2, 4)
DECODE_PAGES = 8
FLASH_BLOCK = 512
WBUILD_UNROLL = 8
EXPERT_TILE = 1024
QUERY_SCALE = HEAD_DK ** -0.5 * math.log2(math.e)
SAFE_SCORE_BOUND = 60.0


def _params(sem):
    return pltpu.CompilerParams(dimension_semantics=sem, vmem_limit_bytes=VMEM_LIMIT)


def _nt_dot(a, b):
    return lax.dot_general(a, b, (((1,), (1,)), ((), ())), preferred_element_type=F32)


def _in_proj_kernel(x_ref, g_ref, w_ref, o_ref, h_ref):
    @pl.when(pl.program_id(1) == 0)
    def _():
        x = x_ref[...]
        ms = jnp.mean(x * x, axis=-1, keepdims=True)
        h_ref[...] = (x * lax.rsqrt(ms + RMS_EPS) * g_ref[...]).astype(BF16)

    o_ref[...] = jnp.dot(h_ref[...], w_ref[...], preferred_element_type=F32)


def _in_proj(x, g, w_bf, tm, tn):
    n, d = x.shape
    cols = w_bf.shape[1]
    return pl.pallas_call(
        _in_proj_kernel,
        grid=(n // tm, cols // tn),
        in_specs=[
            pl.BlockSpec((tm, d), lambda i, j: (i, 0)),
            pl.BlockSpec((1, d), lambda i, j: (0, 0)),
            pl.BlockSpec((d, tn), lambda i, j: (0, j)),
        ],
        out_specs=pl.BlockSpec((tm, tn), lambda i, j: (i, j)),
        out_shape=jax.ShapeDtypeStruct((n, cols), F32),
        scratch_shapes=[pltpu.VMEM((tm, d), BF16)],
        compiler_params=_params(("parallel", "arbitrary")),
        name="in_proj",
    )(x, g, w_bf)


def _qkv_prep_kernel(zq_ref, zk_ref, zv_ref, gq_ref, gk_ref, bd_ref, c_ref, s1_ref, s2_ref,
                     q_ref, kf_ref, kb_ref, vb_ref):
    bd = bd_ref[...]
    cos, sin_lo, sin_hi = c_ref[...], s1_ref[...], s2_ref[...]

    def norm_rope(x, g):
        x2 = x * x
        hi = x2.astype(BF16)
        lo = (x2 - hi.astype(F32)).astype(BF16)
        ss = (jnp.dot(hi, bd, preferred_element_type=F32)
              + jnp.dot(lo, bd, preferred_element_type=F32))
        xn = x * lax.rsqrt(ss * (1.0 / HEAD_DK) + RMS_EPS) * g
        fwd = pltpu.roll(xn, LANES - ROPE_DIMS // 2, axis=1)
        bwd = pltpu.roll(xn, ROPE_DIMS // 2, axis=1)
        return xn * cos + fwd * sin_lo + bwd * sin_hi

    for c in range(QK_COLS // LANES):
        sl = slice(c * LANES, (c + 1) * LANES)
        q = norm_rope(zq_ref[:, sl], gq_ref[...])
        q_ref[:, sl] = (q * QUERY_SCALE).astype(BF16)
        k = norm_rope(zk_ref[:, sl], gk_ref[...])
        kf_ref[:, sl] = k
        kb_ref[:, sl] = k.astype(BF16)
    vb_ref[...] = zv_ref[...].astype(BF16)


def _qkv_prep(z, gq, gk, bd, cos, sin_lo, sin_hi, tm):
    n = z.shape[0]
    nt = cos.shape[0] // tm
    row = lambda c: pl.BlockSpec((tm, QK_COLS), lambda i: (i, c))
    vec = pl.BlockSpec((1, LANES), lambda i: (0, 0))
    tab = pl.BlockSpec((tm, LANES), lambda i: (i % nt, 0))
    out = pl.BlockSpec((tm, QK_COLS), lambda i: (i, 0))
    return pl.pallas_call(
        _qkv_prep_kernel,
        grid=(n // tm,),
        in_specs=[row(0), row(1), row(2), vec, vec,
                  pl.BlockSpec((LANES, LANES), lambda i: (0, 0)), tab, tab, tab],
        out_specs=[out, out, out, out],
        out_shape=[jax.ShapeDtypeStruct((n, QK_COLS), BF16),
                   jax.ShapeDtypeStruct((n, QK_COLS), F32),
                   jax.ShapeDtypeStruct((n, QK_COLS), BF16),
                   jax.ShapeDtypeStruct((n, ATTN_WIDTH), BF16)],
        compiler_params=_params(("parallel",)),
        name="qkv_prep",
    )(z, z, z, gq, gk, bd, cos, sin_lo, sin_hi)


def _rope_tables(pos):
    half = ROPE_DIMS // 2
    inv_freq = ROPE_THETA ** (-jnp.arange(half, dtype=F32) / half)
    ang = pos.astype(F32)[:, None] * inv_freq[None, :]
    cos, sin = jnp.cos(ang), jnp.sin(ang)
    ones = jnp.ones((pos.shape[0], HEAD_DK - ROPE_DIMS), F32)
    zeros8 = jnp.zeros_like(sin)
    zeros = jnp.zeros_like(ones)
    c = jnp.concatenate([cos, cos, ones], axis=1)
    s_lo = jnp.concatenate([-sin, zeros8, zeros], axis=1)
    s_hi = jnp.concatenate([zeros8, sin, zeros], axis=1)
    two = lambda a: jnp.concatenate([a, a], axis=1)
    return two(c), two(s_lo), two(s_hi)


def _head_finish(o, g, lam_init):
    ms = jnp.mean(o * o, axis=-1, keepdims=True)
    return o * lax.rsqrt(ms + RMS_EPS) * g * (1.0 - lam_init)


def _flash_kernel(lam_ref, q_ref, k_ref, v_ref, g_ref, o_ref, *, blk, lam_init):
    qi = pl.program_id(2)
    q = q_ref[0]
    lane = lax.broadcasted_iota(jnp.int32, q.shape, 1)
    zero = jnp.zeros_like(q)
    qq = jnp.concatenate([jnp.where(lane < HEAD_DK, q, zero),
                          jnp.where(lane >= HEAD_DK, q, zero)], axis=0)

    lam, bound = lam_ref[0], lam_ref[1]

    def scores(ki, masked):
        start = pl.multiple_of(ki * blk, blk)
        s = _nt_dot(qq, k_ref[0, pl.ds(start, blk), :])
        if masked:
            r = lax.broadcasted_iota(jnp.int32, s.shape, 0)
            c = lax.broadcasted_iota(jnp.int32, s.shape, 1)
            s = jnp.where(jnp.where(r >= blk, r - blk, r) >= c, s, NEG_INF)
        return s, v_ref[0, pl.ds(start, blk), :]

    def finish(l, acc):
        o = acc / l
        o = o[:blk] - lam * o[blk:]
        o_ref[0] = _head_finish(o, g_ref[...], lam_init).astype(BF16)

    zeros = (jnp.zeros((2 * blk, 1), F32), jnp.zeros((2 * blk, HEAD_DV), F32))

    @pl.when(bound <= SAFE_SCORE_BOUND)
    def _():
        def step(ki, carry, masked):
            l, acc = carry
            s, v = scores(ki, masked)
            p = jnp.exp2(s - bound)
            return (l + jnp.sum(p, axis=1, keepdims=True),
                    acc + jnp.dot(p.astype(BF16), v, preferred_element_type=F32))

        carry = lax.fori_loop(0, qi, lambda ki, c: step(ki, c, False), zeros)
        finish(*step(qi, carry, True))

    @pl.when(bound > SAFE_SCORE_BOUND)
    def _():
        def step(ki, carry, masked):
            m, l, acc = carry
            s, v = scores(ki, masked)
            m_new = jnp.maximum(m, jnp.max(s, axis=1, keepdims=True))
            alpha = jnp.exp2(m - m_new)
            p = jnp.exp2(s - m_new)
            return (m_new, alpha * l + jnp.sum(p, axis=1, keepdims=True),
                    alpha * acc + jnp.dot(p.astype(BF16), v, preferred_element_type=F32))

        init = (jnp.full((2 * blk, 1), NEG_INF, F32),) + zeros
        carry = lax.fori_loop(0, qi, lambda ki, c: step(ki, c, False), init)
        finish(*step(qi, carry, True)[1:])


def _flash_attention(q, k, v, g_head, lam, lam_init, batch, seq, blk):
    q3, k3, v3 = (t.reshape(batch, seq, QK_COLS) for t in (q, k, v))
    whole = pl.BlockSpec((1, seq, LANES), lambda b, h, i: (b, 0, h))
    out = pl.pallas_call(
        functools.partial(_flash_kernel, blk=blk, lam_init=lam_init),
        grid=(batch, N_HEADS, seq // blk),
        in_specs=[
            pl.BlockSpec(memory_space=pltpu.SMEM),
            pl.BlockSpec((1, blk, LANES), lambda b, h, i: (b, i, h)),
            whole, whole,
            pl.BlockSpec((1, LANES), lambda b, h, i: (0, 0)),
        ],
        out_specs=pl.BlockSpec((1, blk, LANES), lambda b, h, i: (b, i, h)),
        out_shape=jax.ShapeDtypeStruct((batch, seq, ATTN_WIDTH), BF16),
        compiler_params=_params(("parallel", "parallel", "arbitrary")),
        name="flash_attention",
    )(lam, q3, k3, v3, g_head)
    return out.reshape(batch * seq, ATTN_WIDTH)


def _decode_kernel(pt_ref, lam_ref, qt_ref, bias_ref, biasn_ref, kn_ref, vn_ref, g_ref, *rest,
                   n_rows, lam_init):
    k_refs, v_refs = rest[:DECODE_PAGES], rest[DECODE_PAGES:2 * DECODE_PAGES]
    o_ref, m_ref, l_ref, acc_ref, s_ref = rest[2 * DECODE_PAGES:]
    step = pl.program_id(1)
    is_last = step == pl.num_programs(1) - 1
    page_rows = PAGE_SIZE * N_HEADS

    @pl.when(step == 0)
    def _():
        m_ref[...] = jnp.full_like(m_ref, NEG_INF)
        l_ref[...] = jnp.zeros_like(l_ref)
        acc_ref[...] = jnp.zeros_like(acc_ref)

    eye = (lax.broadcasted_iota(jnp.int32, (LANES, LANES), 0)
           == lax.broadcasted_iota(jnp.int32, (LANES, LANES), 1))

    def to_col(row):
        return jnp.sum(jnp.where(eye, jnp.broadcast_to(row, (LANES, LANES)), 0.0), axis=1, keepdims=True)

    def contract_rows(p, v):
        return lax.dot_general(p.astype(BF16), v, (((0,), (0,)), ((), ())), preferred_element_type=F32)

    qt = qt_ref[0]
    bias = bias_ref[...]
    m_prev = m_ref[...]
    m_new = m_prev
    for r in range(DECODE_PAGES):
        k2d = k_refs[r][...].reshape(page_rows, LANES).astype(BF16)
        s = jnp.dot(k2d, qt, preferred_element_type=F32)
        s = (s.reshape(PAGE_SIZE, N_HEADS, LANES) + bias).reshape(page_rows, LANES)
        s_ref[r] = s
        m_new = jnp.maximum(m_new, jnp.max(s, axis=0, keepdims=True))
    s_new = (jnp.dot(kn_ref[0], qt, preferred_element_type=F32) + biasn_ref[...]
             + jnp.where(is_last, 0.0, NEG_INF))
    m_new = jnp.maximum(m_new, jnp.max(s_new, axis=0, keepdims=True))

    alpha = jnp.exp2(m_prev - m_new)
    p_new = jnp.exp2(s_new - m_new)
    l = alpha * l_ref[...] + jnp.sum(p_new, axis=0, keepdims=True)
    pv = contract_rows(p_new, vn_ref[0])
    for r in range(DECODE_PAGES):
        p = jnp.exp2(s_ref[r] - m_new)
        l = l + jnp.sum(p, axis=0, keepdims=True)
        pv = pv + contract_rows(p, v_refs[r][...].reshape(page_rows, LANES).astype(BF16))
    acc = to_col(alpha) * acc_ref[...] + pv
    acc_ref[...] = acc
    m_ref[...] = m_new
    l_ref[...] = l

    @pl.when(is_last)
    def _():
        o = acc / to_col(l)
        o = o[:n_rows] - lam_ref[0] * o[n_rows:2 * n_rows]
        o_ref[0] = _head_finish(o, g_ref[...], lam_init).astype(BF16)


def _decode_attention(q, k_new, v_new, cache_k, cache_v, page_table, g_head, lam, lam_init, n_dec, n_new):
    n_pages = page_table.shape[1]
    n_rows = N_HEADS * n_new
    q5 = q.reshape(n_dec, n_new, N_HEADS, 2, HEAD_DK)
    qt = jnp.einsum("bthmd,mM->bMdmht", q5, jnp.eye(2, dtype=BF16)).reshape(n_dec, LANES, 2 * n_rows)
    qt = jnp.pad(qt, ((0, 0), (0, 0), (0, LANES - 2 * n_rows)))
    col = jnp.arange(LANES)
    col_ok, col_head, col_t = col < 2 * n_rows, (col % n_rows) // n_new, col % n_new
    row = jnp.arange(N_HEADS)
    bias = jnp.where(col_ok[None] & (row[:, None] == col_head[None]), 0.0, NEG_INF).astype(F32)
    row_n = jnp.arange(n_rows)
    bias_new = jnp.where(col_ok[None] & (row_n[:, None] % N_HEADS == col_head[None])
                         & (row_n[:, None] // N_HEADS <= col_t[None]), 0.0, NEG_INF).astype(F32)
    kn = k_new.reshape(n_dec, n_rows, LANES)
    vn = v_new.reshape(n_dec, n_rows, LANES)

    def page_spec(r):
        return pl.BlockSpec((None, PAGE_SIZE, N_HEADS, LANES),
                            lambda b, s, pt: (pt[b * n_pages + s * DECODE_PAGES + r], 0, 0, 0))

    const = lambda shape: pl.BlockSpec(shape, lambda b, s, pt: (0,) * len(shape))
    per_b = lambda shape: pl.BlockSpec(shape, lambda b, s, pt: (b,) + (0,) * (len(shape) - 1))
    pages = [page_spec(r) for r in range(DECODE_PAGES)]
    grid_spec = pltpu.PrefetchScalarGridSpec(
        num_scalar_prefetch=1,
        grid=(n_dec, n_pages // DECODE_PAGES),
        in_specs=[pl.BlockSpec(memory_space=pltpu.SMEM), per_b((1, LANES, LANES)),
                  const((N_HEADS, LANES)), const((n_rows, LANES)),
                  per_b((1, n_rows, LANES)), per_b((1, n_rows, LANES)), const((1, LANES))]
                 + pages + pages,
        out_specs=per_b((1, n_rows, HEAD_DV)),
        scratch_shapes=[pltpu.VMEM((1, LANES), F32), pltpu.VMEM((1, LANES), F32),
                        pltpu.VMEM((LANES, HEAD_DV), F32),
                        pltpu.VMEM((DECODE_PAGES, PAGE_SIZE * N_HEADS, LANES), F32)],
    )
    out = pl.pallas_call(
        functools.partial(_decode_kernel, n_rows=n_rows, lam_init=lam_init),
        grid_spec=grid_spec,
        out_shape=jax.ShapeDtypeStruct((n_dec, n_rows, HEAD_DV), BF16),
        compiler_params=_params(("parallel", "arbitrary")),
        name="decode_attention",
    )(page_table.reshape(-1), lam, qt, bias, bias_new, kn, vn, g_head,
      *([cache_k] * DECODE_PAGES), *([cache_v] * DECODE_PAGES))
    out = out.reshape(n_dec, N_HEADS, n_new, HEAD_DV).transpose(0, 2, 1, 3)
    return out.reshape(n_dec * n_new, ATTN_WIDTH)


def _ssm_tables(a_re, a_im, log_dt, b_re, b_im, c_re, c_im, d_skip):
    dt = jnp.exp(log_dt)[:, None]
    mag = jnp.exp(a_re * dt)
    ab_re, ab_im = mag * jnp.cos(a_im * dt), mag * jnp.sin(a_im * dt)
    den = a_re * a_re + a_im * a_im
    f_re = ((ab_re - 1.0) * a_re + ab_im * a_im) / den
    f_im = (ab_im * a_re - (ab_re - 1.0) * a_im) / den
    bb_re = f_re[..., None] * b_re - f_im[..., None] * b_im
    bb_im = f_re[..., None] * b_im + f_im[..., None] * b_re
    eye = jnp.eye(SSM_GB, dtype=F32)

    def b_blocks(bb):
        bb = bb.reshape(SSM_NB, SSM_GB, SSM_STATE, SSM_GROUP)
        return jnp.einsum("agpc,gh->agchp", bb, eye).reshape(SSM_NB, LANES, SSM_SW)

    def c_blocks(cc):
        cc = cc.reshape(SSM_NB, SSM_GB, SSM_GROUP, SSM_STATE)
        return jnp.einsum("agcp,gh->agphc", cc, eye).reshape(SSM_NB, SSM_SW, LANES)

    bbd = jnp.concatenate([b_blocks(bb_re), b_blocks(bb_im)], axis=2).astype(BF16)
    cbd = jnp.concatenate([c_blocks(c_re), c_blocks(-c_im)], axis=1).astype(BF16)
    d = d_skip.reshape(SSM_NB, 1, LANES)
    pr, pi = ab_re[None], ab_im[None]
    while pr.shape[0] < SUBLANES:
        tr, ti = pr[-1:], pi[-1:]
        pr, pi = (jnp.concatenate([pr, pr * tr - pi * ti], axis=0),
                  jnp.concatenate([pi, pr * ti + pi * tr], axis=0))
    lay = lambda t: t.reshape(t.shape[0], SSM_NB, SSM_SW).transpose(1, 0, 2)
    apow = jnp.concatenate([lay(pr), lay(pi)], axis=2)
    rows = jnp.arange(SUBLANES)[None, :, None]
    steps = [jnp.where(rows >= dd, apow[:, dd - 1:dd, :], 0.0) for dd in SCAN_STEPS]
    amask = jnp.stack(steps, axis=1)
    return bbd, cbd, d, apow, amask


def _ssm_prompt_kernel(u_ref, b_ref, c_ref, d_ref, ap_ref, am_ref, y_ref, sre_ref, sim_ref, h_ref):
    chunk = pl.program_id(2)

    @pl.when(chunk == 0)
    def _():
        h_ref[...] = jnp.zeros_like(h_ref)

    u = u_ref[0]
    length = u.shape[0]
    bu = jnp.dot(u.astype(BF16), b_ref[0], preferred_element_type=F32)
    pr, pi = ap_ref[0, :, :SSM_SW], ap_ref[0, :, SSM_SW:]
    cr, ci = h_ref[0:1, :SSM_SW], h_ref[0:1, SSM_SW:]
    out_re, out_im = [], []
    for j in range(length // SUBLANES):
        rows = slice(j * SUBLANES, (j + 1) * SUBLANES)
        re, im = bu[rows, :SSM_SW], bu[rows, SSM_SW:]
        for si, dd in enumerate(SCAN_STEPS):
            mr, mi = am_ref[0, si, :, :SSM_SW], am_ref[0, si, :, SSM_SW:]
            sr, s_i = pltpu.roll(re, dd, axis=0), pltpu.roll(im, dd, axis=0)
            re, im = re + mr * sr - mi * s_i, im + mr * s_i + mi * sr
        re, im = re + pr * cr - pi * ci, im + pr * ci + pi * cr
        cr, ci = re[SUBLANES - 1:SUBLANES], im[SUBLANES - 1:SUBLANES]
        out_re.append(re)
        out_im.append(im)
    h_ref[0:1, :SSM_SW] = cr
    h_ref[0:1, SSM_SW:] = ci
    hcat = jnp.concatenate([jnp.concatenate(out_re, axis=0), jnp.concatenate(out_im, axis=0)],
                           axis=1).astype(BF16)
    y = jnp.dot(hcat, c_ref[0], preferred_element_type=F32) + d_ref[0] * u
    y_ref[0] = jax.nn.gelu(y).astype(BF16)

    @pl.when(chunk == pl.num_programs(2) - 1)
    def _():
        sre_ref[0, 0] = cr
        sim_ref[0, 0] = ci


def _ssm_prompt(z, tables, batch, seq):
    bbd, cbd, d, apow, amask = tables
    length = SSM_CHUNK
    z3 = z.reshape(batch, seq, IN_COLS)
    u_blk = V_END // LANES
    y, sre, sim = pl.pallas_call(
        _ssm_prompt_kernel,
        grid=(batch, SSM_NB, seq // length),
        in_specs=[
            pl.BlockSpec((1, length, LANES), lambda b, g, c: (b, c, u_blk + g)),
            pl.BlockSpec((1, LANES, 2 * SSM_SW), lambda b, g, c: (g, 0, 0)),
            pl.BlockSpec((1, 2 * SSM_SW, LANES), lambda b, g, c: (g, 0, 0)),
            pl.BlockSpec((1, 1, LANES), lambda b, g, c: (g, 0, 0)),
            pl.BlockSpec((1, SUBLANES, 2 * SSM_SW), lambda b, g, c: (g, 0, 0)),
            pl.BlockSpec((1, len(SCAN_STEPS), SUBLANES, 2 * SSM_SW), lambda b, g, c: (g, 0, 0, 0)),
        ],
        out_specs=[
            pl.BlockSpec((1, length, LANES), lambda b, g, c: (b, c, g)),
            pl.BlockSpec((1, 1, 1, SSM_SW), lambda b, g, c: (b, g, 0, 0)),
            pl.BlockSpec((1, 1, 1, SSM_SW), lambda b, g, c: (b, g, 0, 0)),
        ],
        out_shape=[jax.ShapeDtypeStruct((batch, seq, SSM_WIDTH), BF16),
                   jax.ShapeDtypeStruct((batch, SSM_NB, 1, SSM_SW), F32),
                   jax.ShapeDtypeStruct((batch, SSM_NB, 1, SSM_SW), F32)],
        scratch_shapes=[pltpu.VMEM((SUBLANES, 2 * SSM_SW), F32)],
        compiler_params=_params(("parallel", "parallel", "arbitrary")),
        name="ssm_prompt",
    )(z3, bbd, cbd, d, apow, amask)
    state = lambda s: s.reshape(batch, SSM_GROUPS, SSM_STATE)
    return y.reshape(batch * seq, SSM_WIDTH), state(sre), state(sim)


def _ssm_step_kernel(u_ref, b_ref, c_ref, d_ref, ap_ref, hre_ref, him_ref, y_ref, sre_ref, sim_ref):
    ar, ai = ap_ref[0, 0:1, :SSM_SW], ap_ref[0, 0:1, SSM_SW:]
    re, im = hre_ref[...], him_ref[...]
    bmat = b_ref[0]
    for t in range(u_ref.shape[0]):
        u = u_ref[t]
        u_hi = u.astype(BF16)
        u_lo = (u - u_hi.astype(F32)).astype(BF16)
        bu = (jnp.dot(u_hi, bmat, preferred_element_type=F32)
              + jnp.dot(u_lo, bmat, preferred_element_type=F32))
        re, im = (ar * re - ai * im + bu[:, :SSM_SW], ar * im + ai * re + bu[:, SSM_SW:])
        hcat = jnp.concatenate([re, im], axis=1).astype(BF16)
        y = jnp.dot(hcat, c_ref[0], preferred_element_type=F32) + d_ref[0] * u
        y_ref[t] = jax.nn.gelu(y).astype(BF16)
    sre_ref[...] = re
    sim_ref[...] = im


def _ssm_step(z, tables, h_re, h_im, batch, steps):
    bbd, cbd, d, apow, _ = tables
    u = z[:, V_END:U_END].reshape(batch, steps, SSM_WIDTH).transpose(1, 0, 2)
    flat = lambda s: s.reshape(batch, SSM_GROUPS * SSM_STATE)
    y, sre, sim = pl.pallas_call(
        _ssm_step_kernel,
        grid=(SSM_NB,),
        in_specs=[
            pl.BlockSpec((steps, batch, LANES), lambda g: (0, 0, g)),
            pl.BlockSpec((1, LANES, 2 * SSM_SW), lambda g: (g, 0, 0)),
            pl.BlockSpec((1, 2 * SSM_SW, LANES), lambda g: (g, 0, 0)),
            pl.BlockSpec((1, 1, LANES), lambda g: (g, 0, 0)),
            pl.BlockSpec((1, SUBLANES, 2 * SSM_SW), lambda g: (g, 0, 0)),
            pl.BlockSpec((batch, SSM_SW), lambda g: (0, g)),
            pl.BlockSpec((batch, SSM_SW), lambda g: (0, g)),
        ],
        out_specs=[
            pl.BlockSpec((steps, batch, LANES), lambda g: (0, 0, g)),
            pl.BlockSpec((batch, SSM_SW), lambda g: (0, g)),
            pl.BlockSpec((batch, SSM_SW), lambda g: (0, g)),
        ],
        out_shape=[jax.ShapeDtypeStruct((steps, batch, SSM_WIDTH), BF16),
                   jax.ShapeDtypeStruct((batch, SSM_GROUPS * SSM_STATE), F32),
                   jax.ShapeDtypeStruct((batch, SSM_GROUPS * SSM_STATE), F32)],
        compiler_params=_params(("parallel",)),
        name="ssm_step",
    )(u, bbd, cbd, d, apow, flat(h_re), flat(h_im))
    y = y.transpose(1, 0, 2).reshape(batch * steps, SSM_WIDTH)
    state = lambda s: s.reshape(batch, SSM_GROUPS, SSM_STATE)
    return y, state(sre), state(sim)


def _merge_kernel(att_ref, ys_ref, za_ref, zs_ref, wup_ref, wga_ref, wgb_ref, o_ref):
    ys = ys_ref[...]
    branch_a = jnp.dot(att_ref[...], wup_ref[...], preferred_element_type=F32)
    glu_a = jnp.dot(ys, wga_ref[...], preferred_element_type=F32)
    glu_b = jnp.dot(ys, wgb_ref[...], preferred_element_type=F32)
    branch_s = glu_a * jax.nn.sigmoid(glu_b)
    merged = jax.nn.sigmoid(za_ref[...]) * branch_a + jax.nn.sigmoid(zs_ref[...]) * branch_s
    o_ref[...] = merged.astype(BF16)


def _merge(att, ys, z, w_up, w_glu, tm, tn):
    n = att.shape[0]
    nj = D_MODEL // tn
    act = pl.BlockSpec((tm, ATTN_WIDTH), lambda j, i: (i, 0))
    gate = lambda off: pl.BlockSpec((tm, tn), lambda j, i: (i, off + j))
    wcol = lambda off: pl.BlockSpec((ATTN_WIDTH, tn), lambda j, i: (0, off + j))
    return pl.pallas_call(
        _merge_kernel,
        grid=(nj, n // tm),
        in_specs=[act, act, gate(U_END // tn), gate((U_END + D_MODEL) // tn),
                  wcol(0), wcol(0), wcol(nj)],
        out_specs=pl.BlockSpec((tm, tn), lambda j, i: (i, j)),
        out_shape=jax.ShapeDtypeStruct((n, D_MODEL), BF16),
        compiler_params=_params(("parallel", "parallel")),
        name="merge",
    )(att, ys, z, z, w_up, w_glu, w_glu)


def _out_proj_kernel(x_ref, m_ref, w_ref, g_ref, x1_ref, hn_ref):
    x1 = x_ref[...] + jnp.dot(m_ref[...], w_ref[...], preferred_element_type=F32)
    x1_ref[...] = x1
    ms = jnp.mean(x1 * x1, axis=-1, keepdims=True)
    hn_ref[...] = (x1 * lax.rsqrt(ms + RMS_EPS) * g_ref[...]).astype(BF16)


def _out_proj(x, merged, w_out, g_ffn, tm):
    n = x.shape[0]
    row = pl.BlockSpec((tm, D_MODEL), lambda i: (i, 0))
    return pl.pallas_call(
        _out_proj_kernel,
        grid=(n // tm,),
        in_specs=[row, row, pl.BlockSpec((D_MODEL, D_MODEL), lambda i: (0, 0)),
                  pl.BlockSpec((1, D_MODEL), lambda i: (0, 0))],
        out_specs=[row, row],
        out_shape=[jax.ShapeDtypeStruct((n, D_MODEL), F32),
                   jax.ShapeDtypeStruct((n, D_MODEL), BF16)],
        compiler_params=_params(("parallel",)),
        name="out_proj",
    )(x, merged, w_out, g_ffn)


def _top_rows(s, k):
    n = s.shape[0]
    row = lax.broadcasted_iota(jnp.int32, s.shape, 0).astype(F32)
    vals, idxs = [], []
    for _ in range(k):
        best = jnp.max(s, axis=0, keepdims=True)
        pick = jnp.min(jnp.where(s == best, row, float(n)), axis=0, keepdims=True)
        vals.append(best)
        idxs.append(pick)
        s = jnp.where(row == pick, -jnp.inf, s)
    return jnp.concatenate(vals, axis=0), jnp.concatenate(idxs, axis=0)


def _router_kernel(h_ref, wq_ref, sk_ref, a_ref, b_ref, g_ref):
    tm = h_ref.shape[0]
    k = PEER_TOPK
    q = jnp.dot(h_ref[...], wq_ref[...], preferred_element_type=F32).astype(BF16)
    half = PEER_DKEY // 2

    n_rows = k + 8 * SUBLANES
    r = lax.broadcasted_iota(jnp.int32, (n_rows, tm), 0)
    p_mid = ((r - k) >> 3) + 1
    q_mid = (r - k) & 7
    valid = (r < k) | (r >= k + 7 * SUBLANES) | ((p_mid + 1) * (q_mid + 1) <= k)
    r16 = lax.broadcasted_iota(jnp.int32, (k, tm), 0)

    def cand_rows(first, second):
        parts = [first[0:1] + second[0:k]]
        parts += [first[p:p + 1] + second[0:SUBLANES] for p in range(1, 8)]
        parts.append(first[8:16] + second[0:1])
        return jnp.concatenate(parts, axis=0)

    for h in range(PEER_HEADS):
        tops = []
        for m in range(2):
            c = 2 * h + m
            s = _nt_dot(sk_ref[m], q[:, c * half:(c + 1) * half])
            tops.append(_top_rows(s, k))
        (v1, i1), (v2, i2) = tops
        cand = jnp.where(valid, cand_rows(v1, v2), -jnp.inf)
        fv, frow = _top_rows(cand, k)
        frow = frow.astype(jnp.int32)
        tail = k + 7 * SUBLANES
        p_pos = jnp.where(frow < k, 0, jnp.where(frow >= tail, frow - tail + SUBLANES, ((frow - k) >> 3) + 1))
        q_pos = jnp.where(frow < k, frow, jnp.where(frow >= tail, 0, (frow - k) & 7))
        sel_a, sel_b = [], []
        for j in range(k):
            sel_a.append(jnp.max(jnp.where(r16 == p_pos[j:j + 1], i1, -1.0), axis=0, keepdims=True))
            sel_b.append(jnp.max(jnp.where(r16 == q_pos[j:j + 1], i2, -1.0), axis=0, keepdims=True))
        e = jnp.exp(fv - fv[0:1])
        gate = e / jnp.sum(e, axis=0, keepdims=True)
        a_ref[h * k:(h + 1) * k, :] = jnp.concatenate(sel_a, axis=0).astype(jnp.int32)
        b_ref[h * k:(h + 1) * k, :] = jnp.concatenate(sel_b, axis=0).astype(jnp.int32)
        g_ref[h * k:(h + 1) * k, :] = gate


def _router(hn, w_query, sub_keys, tm):
    n = hn.shape[0]
    slots = PEER_HEADS * PEER_TOPK
    out = pl.BlockSpec((slots, tm), lambda i: (0, i))
    return pl.pallas_call(
        _router_kernel,
        grid=(n // tm,),
        in_specs=[pl.BlockSpec((tm, D_MODEL), lambda i: (i, 0)),
                  pl.BlockSpec((D_MODEL, PEER_HEADS * PEER_DKEY), lambda i: (0, 0)),
                  pl.BlockSpec((2, PEER_KEYS, PEER_DKEY // 2), lambda i: (0, 0, 0))],
        out_specs=[out, out, out],
        out_shape=[jax.ShapeDtypeStruct((slots, n), jnp.int32),
                   jax.ShapeDtypeStruct((slots, n), jnp.int32),
                   jax.ShapeDtypeStruct((slots, n), F32)],
        compiler_params=_params(("parallel",)),
        name="router",
    )(hn, w_query, sub_keys)


def _wbuild_kernel(a_ref, b_ref, g_ref, w_ref):
    tm = a_ref.shape[0]
    idx = lax.broadcasted_iota(jnp.int32, (PEER_KEYS, a_ref.shape[1]), 0)
    grouped = (PEER_KEYS // SUBLANES, SUBLANES, PEER_KEYS)
    sub = lax.broadcasted_iota(jnp.int32, grouped, 1)

    def body(i, carry):
        base = pl.multiple_of(i * WBUILD_UNROLL, WBUILD_UNROLL)
        a_rows = a_ref[pl.ds(base, WBUILD_UNROLL), :]
        b_rows = b_ref[pl.ds(base, WBUILD_UNROLL), :]
        g_rows = g_ref[pl.ds(base, WBUILD_UNROLL), :]
        planes = []
        for t in range(WBUILD_UNROLL):
            first = jnp.where(a_rows[t:t + 1] == idx, g_rows[t:t + 1], 0.0).astype(BF16)
            second = jnp.where(b_rows[t:t + 1] == idx, 1.0, 0.0).astype(BF16)
            planes.append(_nt_dot(first, second).reshape(grouped))
        for d in (4, 2, 1):
            upper = (sub & d) != 0
            for t in range(WBUILD_UNROLL):
                if t & d == 0:
                    lo, hi = planes[t], planes[t + d]
                    planes[t] = jnp.where(upper, pltpu.roll(hi, d, axis=1), lo)
                    planes[t + d] = jnp.where(upper, hi, pltpu.roll(lo, SUBLANES - d, axis=1))
        for r in range(WBUILD_UNROLL):
            w_ref[i, :, r] = planes[r]
        return carry

    lax.fori_loop(0, tm // WBUILD_UNROLL, body, 0)


def _wbuild(a, b, g, tm):
    n, slots = a.shape
    row = pl.BlockSpec((tm, slots), lambda i: (i, 0))
    key_hi = PEER_KEYS // SUBLANES
    w = pl.pallas_call(
        _wbuild_kernel,
        grid=(n // tm,),
        in_specs=[row, row, row],
        out_specs=pl.BlockSpec((tm // WBUILD_UNROLL, key_hi, SUBLANES, WBUILD_UNROLL, PEER_KEYS),
                               lambda i: (i, 0, 0, 0, 0)),
        out_shape=jax.ShapeDtypeStruct((n // WBUILD_UNROLL, key_hi, SUBLANES, WBUILD_UNROLL, PEER_KEYS), F32),
        compiler_params=_params(("parallel",)),
        name="wbuild",
    )(a, b, g)
    return w.reshape(n // WBUILD_UNROLL, PEER_KEYS, WBUILD_UNROLL, PEER_KEYS)


def _experts_kernel(h_ref, ut_ref, v_ref, w_ref, o_ref):
    @pl.when(pl.program_id(1) == 0)
    def _():
        o_ref[...] = jnp.zeros_like(o_ref)

    tm = h_ref.shape[0]
    s = jnp.dot(h_ref[...], ut_ref[...], preferred_element_type=F32)
    w = jnp.concatenate([w_ref[:, r].reshape(tm, PEER_KEYS) for r in range(w_ref.shape[1])], axis=1)
    c = (jax.nn.gelu(s) * w).astype(BF16)
    o_ref[...] += jnp.dot(c, v_ref[...], preferred_element_type=F32)


def _experts(hn, u_t, v, w, tm):
    n = hn.shape[0]
    keys_per_tile = EXPERT_TILE // PEER_KEYS
    return pl.pallas_call(
        _experts_kernel,
        grid=(n // tm, PEER_EXPERTS // EXPERT_TILE),
        in_specs=[pl.BlockSpec((tm, D_MODEL), lambda i, j: (i, 0)),
                  pl.BlockSpec((D_MODEL, EXPERT_TILE), lambda i, j: (0, j)),
                  pl.BlockSpec((EXPERT_TILE, D_MODEL), lambda i, j: (j, 0)),
                  pl.BlockSpec((tm // WBUILD_UNROLL, keys_per_tile, WBUILD_UNROLL, PEER_KEYS),
                               lambda i, j: (i, j, 0, 0))],
        out_specs=pl.BlockSpec((tm, D_MODEL), lambda i, j: (i, 0)),
        out_shape=jax.ShapeDtypeStruct((n, D_MODEL), F32),
        compiler_params=_params(("parallel", "arbitrary")),
        name="experts",
    )(hn, u_t, v, w)


def _ple_kernel(x1_ref, peer_ref, p_ref, wp_ref, wg_ref, g_ref, o_ref):
    x2 = x1_ref[...] + peer_ref[...]
    ms = jnp.mean(x2 * x2, axis=-1, keepdims=True)
    hn = (x2 * lax.rsqrt(ms + RMS_EPS) * g_ref[...]).astype(BF16)
    emb = jnp.dot(p_ref[...].astype(BF16), wp_ref[...], preferred_element_type=F32)
    gate = jax.nn.sigmoid(jnp.dot(hn, wg_ref[...], preferred_element_type=F32))
    o_ref[...] = x2 + emb * gate


def _ple(x1, peer, p, w_ple, w_gate, g_ple, tm):
    n = x1.shape[0]
    row = pl.BlockSpec((tm, D_MODEL), lambda i: (i, 0))
    return pl.pallas_call(
        _ple_kernel,
        grid=(n // tm,),
        in_specs=[row, row, pl.BlockSpec((tm, PLE_DIM), lambda i: (i, 0)),
                  pl.BlockSpec((PLE_DIM, D_MODEL), lambda i: (0, 0)),
                  pl.BlockSpec((D_MODEL, D_MODEL), lambda i: (0, 0)),
                  pl.BlockSpec((1, D_MODEL), lambda i: (0, 0))],
        out_specs=row,
        out_shape=jax.ShapeDtypeStruct((n, D_MODEL), F32),
        compiler_params=_params(("parallel",)),
        name="ple",
    )(x1, peer, p, w_ple, w_gate, g_ple)


def _tile(n, pref):
    return pref if n % pref == 0 else n


def _layer(x, p, pos_rows, attention, ssm, wts):
    n = x.shape[0]
    z = _in_proj(x, wts["g_mix"], wts["w_in"], _tile(n, 1024), 1024)
    cos, sin_lo, sin_hi = _rope_tables(pos_rows)
    q, k_f32, k_bf, v_bf = _qkv_prep(z, wts["g_q"], wts["g_k"], wts["bd"], cos, sin_lo, sin_hi,
                                     _tile(cos.shape[0], 512))
    att = attention(q, k_bf, v_bf)
    ys, s_re, s_im = ssm(z)
    merged = _merge(att, ys, z, wts["w_attn_up"], wts["w_glu"], _tile(n, 512), 1024)
    x1, hn = _out_proj(x, merged, wts["w_out"], wts["g_ffn"], _tile(n, 512))
    a_t, b_t, g_t = _router(hn, wts["peer_w_query"], wts["peer_sub_keys"], 128)
    w = _wbuild(a_t.T, b_t.T, g_t.T, _tile(n, 128))
    peer = _experts(hn, wts["peer_u_t"], wts["peer_v"], w, _tile(n, 512))
    y = _ple(x1, peer, p, wts["w_ple"], wts["w_ple_gate"], wts["g_ple"], _tile(n, 512))
    k_new = k_f32.reshape(n, N_HEADS, 2 * HEAD_DK)
    v_new = z[:, 2 * QK_COLS:V_END].reshape(n, N_HEADS, HEAD_DV)
    return y, k_new, v_new, s_re, s_im


def kernel(x_prompt, x_sample, cache_k, cache_v, state_ssm_re, state_ssm_im, page_table, p_prompt, p_sample, g_mix, w_in, g_q, g_k, lambda_q, lambda_k, g_head, w_attn_up, ssm_a_re, ssm_a_im, ssm_log_dt, ssm_b_re, ssm_b_im, ssm_c_re, ssm_c_im, ssm_d, w_glu, w_out, g_ffn, peer_w_query, peer_sub_keys, peer_u, peer_v, g_ple, w_ple, w_ple_gate):
    depth = w_in.shape[0]
    assert depth == 1
    batch, seq, _ = x_prompt.shape
    n_dec, n_new, _ = x_sample.shape
    past_len = page_table.shape[1] * PAGE_SIZE
    i = 0
    lam_init = 0.8 - 0.6 * math.exp(-0.3 * i)

    row = lambda t: t.reshape(1, -1)
    bd = jnp.kron(jnp.eye(2, dtype=F32), jnp.ones((HEAD_DK, HEAD_DK), F32)).astype(BF16)
    wts = {
        "g_mix": row(g_mix[i]), "w_in": w_in[i].astype(BF16),
        "g_q": row(g_q[i]), "g_k": row(g_k[i]), "bd": bd,
        "w_attn_up": w_attn_up[i].astype(BF16), "w_glu": w_glu[i].astype(BF16),
        "w_out": w_out[i].astype(BF16), "g_ffn": row(g_ffn[i]),
        "peer_w_query": peer_w_query[i].astype(BF16), "peer_sub_keys": peer_sub_keys[i].astype(BF16),
        "peer_u_t": peer_u[i].T.astype(BF16), "peer_v": peer_v[i].astype(BF16),
        "g_ple": row(g_ple[i]), "w_ple": w_ple[i].astype(BF16), "w_ple_gate": w_ple_gate[i].astype(BF16),
    }
    g_head2 = row(g_head[i])
    lq, lk = lambda_q[i].astype(F32), lambda_k[i].astype(F32)
    lam = jnp.exp(jnp.sum(lq[0] * lk[0])) - jnp.exp(jnp.sum(lq[1] * lk[1])) + lam_init
    score_bound = (1.05 * QUERY_SCALE * HEAD_DK) * jnp.max(jnp.abs(g_q[i])) * jnp.max(jnp.abs(g_k[i]))
    lam = jnp.stack([lam, score_bound]).astype(F32)
    ssm_args = (ssm_a_re[i], ssm_a_im[i], ssm_log_dt[i], ssm_b_re[i], ssm_b_im[i],
                ssm_c_re[i], ssm_c_im[i], ssm_d[i])
    tables_prompt = tables_step = _ssm_tables(*ssm_args)

    pos_prompt = jnp.arange(seq, dtype=jnp.int32)
    y_p, k_p, v_p, sre_p, sim_p = _layer(
        x_prompt.reshape(batch * seq, D_MODEL), p_prompt[i].reshape(batch * seq, PLE_DIM), pos_prompt,
        lambda q, k, v: _flash_attention(q, k, v, g_head2, lam, lam_init, batch, seq, FLASH_BLOCK),
        lambda z: _ssm_prompt(z, tables_prompt, batch, seq),
        wts)

    pos_sample = jnp.tile(past_len + jnp.arange(n_new, dtype=jnp.int32), n_dec)
    y_s, k_s, v_s, sre_s, sim_s = _layer(
        x_sample.reshape(n_dec * n_new, D_MODEL), p_sample[i].reshape(n_dec * n_new, PLE_DIM), pos_sample,
        lambda q, k, v: _decode_attention(q, k, v, cache_k[i], cache_v[i], page_table, g_head2, lam,
                                          lam_init, n_dec, n_new),
        lambda z: _ssm_step(z, tables_step, state_ssm_re[i], state_ssm_im[i], n_dec, n_new),
        wts)

    lead = lambda t, *shape: t.reshape(1, *shape)
    return (y_p.reshape(batch, seq, D_MODEL), y_s.reshape(n_dec, n_new, D_MODEL),
            lead(k_p, batch, seq, N_HEADS, 2 * HEAD_DK), lead(v_p, batch, seq, N_HEADS, HEAD_DV),
            lead(sre_p, batch, SSM_GROUPS, SSM_STATE), lead(sim_p, batch, SSM_GROUPS, SSM_STATE),
            lead(k_s, n_dec, n_new, N_HEADS, 2 * HEAD_DK), lead(v_s, n_dec, n_new, N_HEADS, HEAD_DV),
            lead(sre_s, n_dec, SSM_GROUPS, SSM_STATE), lead(sim_s, n_dec, SSM_GROUPS, SSM_STATE))
```

```python
import functools
import math

import jax
import jax.numpy as jnp
from jax import lax
from jax.experimental import pallas as pl
from jax.experimental.pallas import tpu as pltpu

F32 = jnp.float32
BF16 = jnp.bfloat16

D_MODEL = 2048
PAGE_SIZE = 128
N_HEADS = 8
HEAD_DK = 64
HEAD_DV = 2 * HEAD_DK
QK_COLS = N_HEADS * 2 * HEAD_DK
ATTN_WIDTH = N_HEADS * HEAD_DV
ROPE_DIMS = HEAD_DK // 4
ROPE_THETA = 500000.0
NEG_INF = -1e30
SSM_WIDTH = D_MODEL // 2
SSM_GROUP = 16
SSM_GROUPS = SSM_WIDTH // SSM_GROUP
SSM_STATE = 64
PEER_HEADS = 8
PEER_KEYS = 128
PEER_EXPERTS = PEER_KEYS * PEER_KEYS
PEER_DKEY = 256
PEER_TOPK = 16
PLE_DIM = 256
RMS_EPS = 1e-6
V_END = 2 * QK_COLS + ATTN_WIDTH
U_END = V_END + SSM_WIDTH
IN_COLS = U_END + 2 * D_MODEL

LANES = 128
SUBLANES = 8
VMEM_LIMIT = 52 * 1024 * 1024
SSM_GB = LANES // SSM_GROUP
SSM_NB = SSM_GROUPS // SSM_GB
SSM_SW = SSM_GB * SSM_STATE
SSM_CHUNK = 512
SCAN_STEPS = (1, 2, 4)
DECODE_PAGES = 16
FLASH_BLOCK = 512
WBUILD_UNROLL = 8
EXPERT_TILE = 1024
QUERY_SCALE = HEAD_DK ** -0.5 * math.log2(math.e)
SAFE_SCORE_BOUND = 60.0


def _params(sem):
    return pltpu.CompilerParams(dimension_semantics=sem, vmem_limit_bytes=VMEM_LIMIT)


def _nt_dot(a, b):
    return lax.dot_general(a, b, (((1,), (1,)), ((), ())), preferred_element_type=F32)


def _in_proj_kernel(x_ref, g_ref, w_ref, o_ref, h_ref):
    @pl.when(pl.program_id(1) == 0)
    def _():
        x = x_ref[...]
        ms = jnp.mean(x * x, axis=-1, keepdims=True)
        h_ref[...] = (x * lax.rsqrt(ms + RMS_EPS) * g_ref[...]).astype(BF16)

    o_ref[...] = jnp.dot(h_ref[...], w_ref[...], preferred_element_type=F32)


def _in_proj(x, g, w_bf, tm, tn):
    n, d = x.shape
    cols = w_bf.shape[1]
    return pl.pallas_call(
        _in_proj_kernel,
        grid=(n // tm, cols // tn),
        in_specs=[
            pl.BlockSpec((tm, d), lambda i, j: (i, 0)),
            pl.BlockSpec((1, d), lambda i, j: (0, 0)),
            pl.BlockSpec((d, tn), lambda i, j: (0, j)),
        ],
        out_specs=pl.BlockSpec((tm, tn), lambda i, j: (i, j)),
        out_shape=jax.ShapeDtypeStruct((n, cols), F32),
        scratch_shapes=[pltpu.VMEM((tm, d), BF16)],
        compiler_params=_params(("parallel", "arbitrary")),
        name="in_proj",
    )(x, g, w_bf)


def _qkv_prep_kernel(zq_ref, zk_ref, zv_ref, gq_ref, gk_ref, bd_ref, c_ref, s1_ref, s2_ref,
                     q_ref, kf_ref, kb_ref, vb_ref):
    bd = bd_ref[...]
    cos, sin_lo, sin_hi = c_ref[...], s1_ref[...], s2_ref[...]

    def norm_rope(x, g):
        x2 = x * x
        hi = x2.astype(BF16)
        lo = (x2 - hi.astype(F32)).astype(BF16)
        ss = (jnp.dot(hi, bd, preferred_element_type=F32)
              + jnp.dot(lo, bd, preferred_element_type=F32))
        xn = x * lax.rsqrt(ss * (1.0 / HEAD_DK) + RMS_EPS) * g
        fwd = pltpu.roll(xn, LANES - ROPE_DIMS // 2, axis=1)
        bwd = pltpu.roll(xn, ROPE_DIMS // 2, axis=1)
        return xn * cos + fwd * sin_lo + bwd * sin_hi

    for c in range(QK_COLS // LANES):
        sl = slice(c * LANES, (c + 1) * LANES)
        q = norm_rope(zq_ref[:, sl], gq_ref[...])
        q_ref[:, sl] = (q * QUERY_SCALE).astype(BF16)
        k = norm_rope(zk_ref[:, sl], gk_ref[...])
        kf_ref[:, sl] = k
        kb_ref[:, sl] = k.astype(BF16)
    vb_ref[...] = zv_ref[...].astype(BF16)


def _qkv_prep(z, gq, gk, bd, cos, sin_lo, sin_hi, tm):
    n = z.shape[0]
    nt = cos.shape[0] // tm
    row = lambda c: pl.BlockSpec((tm, QK_COLS), lambda i: (i, c))
    vec = pl.BlockSpec((1, LANES), lambda i: (0, 0))
    tab = pl.BlockSpec((tm, LANES), lambda i: (i % nt, 0))
    out = pl.BlockSpec((tm, QK_COLS), lambda i: (i, 0))
    return pl.pallas_call(
        _qkv_prep_kernel,
        grid=(n // tm,),
        in_specs=[row(0), row(1), row(2), vec, vec,
                  pl.BlockSpec((LANES, LANES), lambda i: (0, 0)), tab, tab, tab],
        out_specs=[out, out, out, out],
        out_shape=[jax.ShapeDtypeStruct((n, QK_COLS), BF16),
                   jax.ShapeDtypeStruct((n, QK_COLS), F32),
                   jax.ShapeDtypeStruct((n, QK_COLS), BF16),
                   jax.ShapeDtypeStruct((n, ATTN_WIDTH), BF16)],
        compiler_params=_params(("parallel",)),
        name="qkv_prep",
    )(z, z, z, gq, gk, bd, cos, sin_lo, sin_hi)


def _rope_tables(pos):
    half = ROPE_DIMS // 2
    inv_freq = ROPE_THETA ** (-jnp.arange(half, dtype=F32) / half)
    ang = pos.astype(F32)[:, None] * inv_freq[None, :]
    cos, sin = jnp.cos(ang), jnp.sin(ang)
    ones = jnp.ones((pos.shape[0], HEAD_DK - ROPE_DIMS), F32)
    zeros8 = jnp.zeros_like(sin)
    zeros = jnp.zeros_like(ones)
    c = jnp.concatenate([cos, cos, ones], axis=1)
    s_lo = jnp.concatenate([-sin, zeros8, zeros], axis=1)
    s_hi = jnp.concatenate([zeros8, sin, zeros], axis=1)
    two = lambda a: jnp.concatenate([a, a], axis=1)
    return two(c), two(s_lo), two(s_hi)


def _head_finish(o, g, lam_init):
    ms = jnp.mean(o * o, axis=-1, keepdims=True)
    return o * lax.rsqrt(ms + RMS_EPS) * g * (1.0 - lam_init)


def _flash_kernel(lam_ref, q_ref, k_ref, v_ref, g_ref, o_ref, *, blk, lam_init):
    qi = pl.program_id(2)
    q = q_ref[0]
    lane = lax.broadcasted_iota(jnp.int32, q.shape, 1)
    zero = jnp.zeros_like(q)
    qq = jnp.concatenate([jnp.where(lane < HEAD_DK, q, zero),
                          jnp.where(lane >= HEAD_DK, q, zero)], axis=0)

    lam, bound = lam_ref[0], lam_ref[1]

    def scores(ki, masked):
        start = pl.multiple_of(ki * blk, blk)
        s = _nt_dot(qq, k_ref[0, pl.ds(start, blk), :])
        if masked:
            r = lax.broadcasted_iota(jnp.int32, s.shape, 0)
            c = lax.broadcasted_iota(jnp.int32, s.shape, 1)
            s = jnp.where(jnp.where(r >= blk, r - blk, r) >= c, s, NEG_INF)
        return s, v_ref[0, pl.ds(start, blk), :]

    def finish(l, acc):
        o = acc / l
        o = o[:blk] - lam * o[blk:]
        o_ref[0] = _head_finish(o, g_ref[...], lam_init).astype(BF16)

    zeros = (jnp.zeros((2 * blk, 1), F32), jnp.zeros((2 * blk, HEAD_DV), F32))

    @pl.when(bound <= SAFE_SCORE_BOUND)
    def _():
        def step(ki, carry, masked):
            l, acc = carry
            s, v = scores(ki, masked)
            p = jnp.exp2(s - bound)
            return (l + jnp.sum(p, axis=1, keepdims=True),
                    acc + jnp.dot(p.astype(BF16), v, preferred_element_type=F32))

        carry = lax.fori_loop(0, qi, lambda ki, c: step(ki, c, False), zeros)
        finish(*step(qi, carry, True))

    @pl.when(bound > SAFE_SCORE_BOUND)
    def _():
        def step(ki, carry, masked):
            m, l, acc = carry
            s, v = scores(ki, masked)
            m_new = jnp.maximum(m, jnp.max(s, axis=1, keepdims=True))
            alpha = jnp.exp2(m - m_new)
            p = jnp.exp2(s - m_new)
            return (m_new, alpha * l + jnp.sum(p, axis=1, keepdims=True),
                    alpha * acc + jnp.dot(p.astype(BF16), v, preferred_element_type=F32))

        init = (jnp.full((2 * blk, 1), NEG_INF, F32),) + zeros
        carry = lax.fori_loop(0, qi, lambda ki, c: step(ki, c, False), init)
        finish(*step(qi, carry, True)[1:])


def _flash_attention(q, k, v, g_head, lam, lam_init, batch, seq, blk):
    q3, k3, v3 = (t.reshape(batch, seq, QK_COLS) for t in (q, k, v))
    whole = pl.BlockSpec((1, seq, LANES), lambda b, h, i: (b, 0, h))
    out = pl.pallas_call(
        functools.partial(_flash_kernel, blk=blk, lam_init=lam_init),
        grid=(batch, N_HEADS, seq // blk),
        in_specs=[
            pl.BlockSpec(memory_space=pltpu.SMEM),
            pl.BlockSpec((1, blk, LANES), lambda b, h, i: (b, i, h)),
            whole, whole,
            pl.BlockSpec((1, LANES), lambda b, h, i: (0, 0)),
        ],
        out_specs=pl.BlockSpec((1, blk, LANES), lambda b, h, i: (b, i, h)),
        out_shape=jax.ShapeDtypeStruct((batch, seq, ATTN_WIDTH), BF16),
        compiler_params=_params(("parallel", "parallel", "arbitrary")),
        name="flash_attention",
    )(lam, q3, k3, v3, g_head)
    return out.reshape(batch * seq, ATTN_WIDTH)


def _decode_kernel(pt_ref, lam_ref, qt_ref, bias_ref, biasn_ref, kn_ref, vn_ref, g_ref, *rest,
                   n_rows, lam_init):
    k_refs, v_refs = rest[:DECODE_PAGES], rest[DECODE_PAGES:2 * DECODE_PAGES]
    o_ref, m_ref, l_ref, acc_ref, s_ref = rest[2 * DECODE_PAGES:]
    step = pl.program_id(1)
    is_last = step == pl.num_programs(1) - 1
    page_rows = PAGE_SIZE * N_HEADS

    @pl.when(step == 0)
    def _():
        m_ref[...] = jnp.full_like(m_ref, NEG_INF)
        l_ref[...] = jnp.zeros_like(l_ref)
        acc_ref[...] = jnp.zeros_like(acc_ref)

    eye = (lax.broadcasted_iota(jnp.int32, (LANES, LANES), 0)
           == lax.broadcasted_iota(jnp.int32, (LANES, LANES), 1))

    def to_col(row):
        return jnp.sum(jnp.where(eye, jnp.broadcast_to(row, (LANES, LANES)), 0.0), axis=1, keepdims=True)

    def contract_rows(p, v):
        return lax.dot_general(p.astype(BF16), v, (((0,), (0,)), ((), ())), preferred_element_type=F32)

    qt = qt_ref[0]
    bias = bias_ref[...]
    m_prev = m_ref[...]
    m_new = m_prev
    for r in range(DECODE_PAGES):
        k2d = k_refs[r][...].reshape(page_rows, LANES).astype(BF16)
        s = jnp.dot(k2d, qt, preferred_element_type=F32)
        s = (s.reshape(PAGE_SIZE, N_HEADS, LANES) + bias).reshape(page_rows, LANES)
        s_ref[r] = s
        m_new = jnp.maximum(m_new, jnp.max(s, axis=0, keepdims=True))
    s_new = (jnp.dot(kn_ref[0], qt, preferred_element_type=F32) + biasn_ref[...]
             + jnp.where(is_last, 0.0, NEG_INF))
    m_new = jnp.maximum(m_new, jnp.max(s_new, axis=0, keepdims=True))

    alpha = jnp.exp2(m_prev - m_new)
    p_new = jnp.exp2(s_new - m_new)
    l = alpha * l_ref[...] + jnp.sum(p_new, axis=0, keepdims=True)
    pv = contract_rows(p_new, vn_ref[0])
    for r in range(DECODE_PAGES):
        p = jnp.exp2(s_ref[r] - m_new)
        l = l + jnp.sum(p, axis=0, keepdims=True)
        pv = pv + contract_rows(p, v_refs[r][...].reshape(page_rows, LANES).astype(BF16))
    acc = to_col(alpha) * acc_ref[...] + pv
    acc_ref[...] = acc
    m_ref[...] = m_new
    l_ref[...] = l

    @pl.when(is_last)
    def _():
        o = acc / to_col(l)
        o = o[:n_rows] - lam_ref[0] * o[n_rows:2 * n_rows]
        o_ref[0] = _head_finish(o, g_ref[...], lam_init).astype(BF16)


def _decode_attention(q, k_new, v_new, cache_k, cache_v, page_table, g_head, lam, lam_init, n_dec, n_new):
    n_pages = page_table.shape[1]
    n_rows = N_HEADS * n_new
    q5 = q.reshape(n_dec, n_new, N_HEADS, 2, HEAD_DK)
    qt = jnp.einsum("bthmd,mM->bMdmht", q5, jnp.eye(2, dtype=BF16)).reshape(n_dec, LANES, 2 * n_rows)
    qt = jnp.pad(qt, ((0, 0), (0, 0), (0, LANES - 2 * n_rows)))
    col = jnp.arange(LANES)
    col_ok, col_head, col_t = col < 2 * n_rows, (col % n_rows) // n_new, col % n_new
    row = jnp.arange(N_HEADS)
    bias = jnp.where(col_ok[None] & (row[:, None] == col_head[None]), 0.0, NEG_INF).astype(F32)
    row_n = jnp.arange(n_rows)
    bias_new = jnp.where(col_ok[None] & (row_n[:, None] % N_HEADS == col_head[None])
                         & (row_n[:, None] // N_HEADS <= col_t[None]), 0.0, NEG_INF).astype(F32)
    kn = k_new.reshape(n_dec, n_rows, LANES)
    vn = v_new.reshape(n_dec, n_rows, LANES)

    def page_spec(r):
        return pl.BlockSpec((None, PAGE_SIZE, N_HEADS, LANES),
                            lambda b, s, pt: (pt[b * n_pages + s * DECODE_PAGES + r], 0, 0, 0))

    const = lambda shape: pl.BlockSpec(shape, lambda b, s, pt: (0,) * len(shape))
    per_b = lambda shape: pl.BlockSpec(shape, lambda b, s, pt: (b,) + (0,) * (len(shape) - 1))
    pages = [page_spec(r) for r in range(DECODE_PAGES)]
    grid_spec = pltpu.PrefetchScalarGridSpec(
        num_scalar_prefetch=1,
        grid=(n_dec, n_pages // DECODE_PAGES),
        in_specs=[pl.BlockSpec(memory_space=pltpu.SMEM), per_b((1, LANES, LANES)),
                  const((N_HEADS, LANES)), const((n_rows, LANES)),
                  per_b((1, n_rows, LANES)), per_b((1, n_rows, LANES)), const((1, LANES))]
                 + pages + pages,
        out_specs=per_b((1, n_rows, HEAD_DV)),
        scratch_shapes=[pltpu.VMEM((1, LANES), F32), pltpu.VMEM((1, LANES), F32),
                        pltpu.VMEM((LANES, HEAD_DV), F32),
                        pltpu.VMEM((DECODE_PAGES, PAGE_SIZE * N_HEADS, LANES), F32)],
    )
    out = pl.pallas_call(
        functools.partial(_decode_kernel, n_rows=n_rows, lam_init=lam_init),
        grid_spec=grid_spec,
        out_shape=jax.ShapeDtypeStruct((n_dec, n_rows, HEAD_DV), BF16),
        compiler_params=_params(("parallel", "arbitrary")),
        name="decode_attention",
    )(page_table.reshape(-1), lam, qt, bias, bias_new, kn, vn, g_head,
      *([cache_k] * DECODE_PAGES), *([cache_v] * DECODE_PAGES))
    out = out.reshape(n_dec, N_HEADS, n_new, HEAD_DV).transpose(0, 2, 1, 3)
    return out.reshape(n_dec * n_new, ATTN_WIDTH)


def _ssm_tables(a_re, a_im, log_dt, b_re, b_im, c_re, c_im, d_skip):
    dt = jnp.exp(log_dt)[:, None]
    mag = jnp.exp(a_re * dt)
    ab_re, ab_im = mag * jnp.cos(a_im * dt), mag * jnp.sin(a_im * dt)
    den = a_re * a_re + a_im * a_im
    f_re = ((ab_re - 1.0) * a_re + ab_im * a_im) / den
    f_im = (ab_im * a_re - (ab_re - 1.0) * a_im) / den
    bb_re = f_re[..., None] * b_re - f_im[..., None] * b_im
    bb_im = f_re[..., None] * b_im + f_im[..., None] * b_re
    eye = jnp.eye(SSM_GB, dtype=F32)

    def b_blocks(bb):
        bb = bb.reshape(SSM_NB, SSM_GB, SSM_STATE, SSM_GROUP)
        return jnp.einsum("agpc,gh->agchp", bb, eye).reshape(SSM_NB, LANES, SSM_SW)

    def c_blocks(cc):
        cc = cc.reshape(SSM_NB, SSM_GB, SSM_GROUP, SSM_STATE)
        return jnp.einsum("agcp,gh->agphc", cc, eye).reshape(SSM_NB, SSM_SW, LANES)

    bbd = jnp.concatenate([b_blocks(bb_re), b_blocks(bb_im)], axis=2).astype(BF16)
    cbd = jnp.concatenate([c_blocks(c_re), c_blocks(-c_im)], axis=1).astype(BF16)
    d = d_skip.reshape(SSM_NB, 1, LANES)
    pr, pi = ab_re[None], ab_im[None]
    while pr.shape[0] < SUBLANES:
        tr, ti = pr[-1:], pi[-1:]
        pr, pi = (jnp.concatenate([pr, pr * tr - pi * ti], axis=0),
                  jnp.concatenate([pi, pr * ti + pi * tr], axis=0))
    lay = lambda t: t.reshape(t.shape[0], SSM_NB, SSM_SW).transpose(1, 0, 2)
    apow = jnp.concatenate([lay(pr), lay(pi)], axis=2)
    rows = jnp.arange(SUBLANES)[None, :, None]
    steps = [jnp.where(rows >= dd, apow[:, dd - 1:dd, :], 0.0) for dd in SCAN_STEPS]
    amask = jnp.stack(steps, axis=1)
    return bbd, cbd, d, apow, amask


def _ssm_prompt_kernel(u_ref, b_ref, c_ref, d_ref, ap_ref, am_ref, y_ref, sre_ref, sim_ref, h_ref):
    chunk = pl.program_id(2)

    @pl.when(chunk == 0)
    def _():
        h_ref[...] = jnp.zeros_like(h_ref)

    u = u_ref[0]
    length = u.shape[0]
    bu = jnp.dot(u.astype(BF16), b_ref[0], preferred_element_type=F32)
    pr, pi = ap_ref[0, :, :SSM_SW], ap_ref[0, :, SSM_SW:]
    cr, ci = h_ref[0:1, :SSM_SW], h_ref[0:1, SSM_SW:]
    out_re, out_im = [], []
    for j in range(length // SUBLANES):
        rows = slice(j * SUBLANES, (j + 1) * SUBLANES)
        re, im = bu[rows, :SSM_SW], bu[rows, SSM_SW:]
        for si, dd in enumerate(SCAN_STEPS):
            mr, mi = am_ref[0, si, :, :SSM_SW], am_ref[0, si, :, SSM_SW:]
            sr, s_i = pltpu.roll(re, dd, axis=0), pltpu.roll(im, dd, axis=0)
            re, im = re + mr * sr - mi * s_i, im + mr * s_i + mi * sr
        re, im = re + pr * cr - pi * ci, im + pr * ci + pi * cr
        cr, ci = re[SUBLANES - 1:SUBLANES], im[SUBLANES - 1:SUBLANES]
        out_re.append(re)
        out_im.append(im)
    h_ref[0:1, :SSM_SW] = cr
    h_ref[0:1, SSM_SW:] = ci
    hcat = jnp.concatenate([jnp.concatenate(out_re, axis=0), jnp.concatenate(out_im, axis=0)],
                           axis=1).astype(BF16)
    y = jnp.dot(hcat, c_ref[0], preferred_element_type=F32) + d_ref[0] * u
    y_ref[0] = jax.nn.gelu(y).astype(BF16)

    @pl.when(chunk == pl.num_programs(2) - 1)
    def _():
        sre_ref[0, 0] = cr
        sim_ref[0, 0] = ci


def _ssm_prompt(z, tables, batch, seq):
    bbd, cbd, d, apow, amask = tables
    length = SSM_CHUNK
    z3 = z.reshape(batch, seq, IN_COLS)
    u_blk = V_END // LANES
    y, sre, sim = pl.pallas_call(
        _ssm_prompt_kernel,
        grid=(batch, SSM_NB, seq // length),
        in_specs=[
            pl.BlockSpec((1, length, LANES), lambda b, g, c: (b, c, u_blk + g)),
            pl.BlockSpec((1, LANES, 2 * SSM_SW), lambda b, g, c: (g, 0, 0)),
            pl.BlockSpec((1, 2 * SSM_SW, LANES), lambda b, g, c: (g, 0, 0)),
            pl.BlockSpec((1, 1, LANES), lambda b, g, c: (g, 0, 0)),
            pl.BlockSpec((1, SUBLANES, 2 * SSM_SW), lambda b, g, c: (g, 0, 0)),
            pl.BlockSpec((1, len(SCAN_STEPS), SUBLANES, 2 * SSM_SW), lambda b, g, c: (g, 0, 0, 0)),
        ],
        out_specs=[
            pl.BlockSpec((1, length, LANES), lambda b, g, c: (b, c, g)),
            pl.BlockSpec((1, 1, 1, SSM_SW), lambda b, g, c: (b, g, 0, 0)),
            pl.BlockSpec((1, 1, 1, SSM_SW), lambda b, g, c: (b, g, 0, 0)),
        ],
        out_shape=[jax.ShapeDtypeStruct((batch, seq, SSM_WIDTH), BF16),
                   jax.ShapeDtypeStruct((batch, SSM_NB, 1, SSM_SW), F32),
                   jax.ShapeDtypeStruct((batch, SSM_NB, 1, SSM_SW), F32)],
        scratch_shapes=[pltpu.VMEM((SUBLANES, 2 * SSM_SW), F32)],
        compiler_params=_params(("parallel", "parallel", "arbitrary")),
        name="ssm_prompt",
    )(z3, bbd, cbd, d, apow, amask)
    state = lambda s: s.reshape(batch, SSM_GROUPS, SSM_STATE)
    return y.reshape(batch * seq, SSM_WIDTH), state(sre), state(sim)


def _ssm_step_kernel(u_ref, b_ref, c_ref, d_ref, ap_ref, hre_ref, him_ref, y_ref, sre_ref, sim_ref):
    ar, ai = ap_ref[0, 0:1, :SSM_SW], ap_ref[0, 0:1, SSM_SW:]
    re, im = hre_ref[...], him_ref[...]
    bmat = b_ref[0]
    for t in range(u_ref.shape[0]):
        u = u_ref[t]
        u_hi = u.astype(BF16)
        u_lo = (u - u_hi.astype(F32)).astype(BF16)
        bu = (jnp.dot(u_hi, bmat, preferred_element_type=F32)
              + jnp.dot(u_lo, bmat, preferred_element_type=F32))
        re, im = (ar * re - ai * im + bu[:, :SSM_SW], ar * im + ai * re + bu[:, SSM_SW:])
        hcat = jnp.concatenate([re, im], axis=1).astype(BF16)
        y = jnp.dot(hcat, c_ref[0], preferred_element_type=F32) + d_ref[0] * u
        y_ref[t] = jax.nn.gelu(y).astype(BF16)
    sre_ref[...] = re
    sim_ref[...] = im


def _ssm_step(z, tables, h_re, h_im, batch, steps):
    bbd, cbd, d, apow, _ = tables
    u = z[:, V_END:U_END].reshape(batch, steps, SSM_WIDTH).transpose(1, 0, 2)
    flat = lambda s: s.reshape(batch, SSM_GROUPS * SSM_STATE)
    y, sre, sim = pl.pallas_call(
        _ssm_step_kernel,
        grid=(SSM_NB,),
        in_specs=[
            pl.BlockSpec((steps, batch, LANES), lambda g: (0, 0, g)),
            pl.BlockSpec((1, LANES, 2 * SSM_SW), lambda g: (g, 0, 0)),
            pl.BlockSpec((1, 2 * SSM_SW, LANES), lambda g: (g, 0, 0)),
            pl.BlockSpec((1, 1, LANES), lambda g: (g, 0, 0)),
            pl.BlockSpec((1, SUBLANES, 2 * SSM_SW), lambda g: (g, 0, 0)),
            pl.BlockSpec((batch, SSM_SW), lambda g: (0, g)),
            pl.BlockSpec((batch, SSM_SW), lambda g: (0, g)),
        ],
        out_specs=[
            pl.BlockSpec((steps, batch, LANES), lambda g: (0, 0, g)),
            pl.BlockSpec((batch, SSM_SW), lambda g: (0, g)),
            pl.BlockSpec((batch, SSM_SW), lambda g: (0, g)),
        ],
        out_shape=[jax.ShapeDtypeStruct((steps, batch, SSM_WIDTH), BF16),
                   jax.ShapeDtypeStruct((batch, SSM_GROUPS * SSM_STATE), F32),
                   jax.ShapeDtypeStruct((batch, SSM_GROUPS * SSM_STATE), F32)],
        compiler_params=_params(("parallel",)),
        name="ssm_step",
    )(u, bbd, cbd, d, apow, flat(h_re), flat(h_im))
    y = y.transpose(1, 0, 2).reshape(batch * steps, SSM_WIDTH)
    state = lambda s: s.reshape(batch, SSM_GROUPS, SSM_STATE)
    return y, state(sre), state(sim)


def _merge_kernel(att_ref, ys_ref, za_ref, zs_ref, wup_ref, wga_ref, wgb_ref, o_ref):
    ys = ys_ref[...]
    branch_a = jnp.dot(att_ref[...], wup_ref[...], preferred_element_type=F32)
    glu_a = jnp.dot(ys, wga_ref[...], preferred_element_type=F32)
    glu_b = jnp.dot(ys, wgb_ref[...], preferred_element_type=F32)
    branch_s = glu_a * jax.nn.sigmoid(glu_b)
    merged = jax.nn.sigmoid(za_ref[...]) * branch_a + jax.nn.sigmoid(zs_ref[...]) * branch_s
    o_ref[...] = merged.astype(BF16)


def _merge(att, ys, z, w_up, w_glu, tm, tn):
    n = att.shape[0]
    nj = D_MODEL // tn
    act = pl.BlockSpec((tm, ATTN_WIDTH), lambda j, i: (i, 0))
    gate = lambda off: pl.BlockSpec((tm, tn), lambda j, i: (i, off + j))
    wcol = lambda off: pl.BlockSpec((ATTN_WIDTH, tn), lambda j, i: (0, off + j))
    return pl.pallas_call(
        _merge_kernel,
        grid=(nj, n // tm),
        in_specs=[act, act, gate(U_END // tn), gate((U_END + D_MODEL) // tn),
                  wcol(0), wcol(0), wcol(nj)],
        out_specs=pl.BlockSpec((tm, tn), lambda j, i: (i, j)),
        out_shape=jax.ShapeDtypeStruct((n, D_MODEL), BF16),
        compiler_params=_params(("parallel", "parallel")),
        name="merge",
    )(att, ys, z, z, w_up, w_glu, w_glu)


def _out_proj_kernel(x_ref, m_ref, w_ref, g_ref, x1_ref, hn_ref):
    x1 = x_ref[...] + jnp.dot(m_ref[...], w_ref[...], preferred_element_type=F32)
    x1_ref[...] = x1
    ms = jnp.mean(x1 * x1, axis=-1, keepdims=True)
    hn_ref[...] = (x1 * lax.rsqrt(ms + RMS_EPS) * g_ref[...]).astype(BF16)


def _out_proj(x, merged, w_out, g_ffn, tm):
    n = x.shape[0]
    row = pl.BlockSpec((tm, D_MODEL), lambda i: (i, 0))
    return pl.pallas_call(
        _out_proj_kernel,
        grid=(n // tm,),
        in_specs=[row, row, pl.BlockSpec((D_MODEL, D_MODEL), lambda i: (0, 0)),
                  pl.BlockSpec((1, D_MODEL), lambda i: (0, 0))],
        out_specs=[row, row],
        out_shape=[jax.ShapeDtypeStruct((n, D_MODEL), F32),
                   jax.ShapeDtypeStruct((n, D_MODEL), BF16)],
        compiler_params=_params(("parallel",)),
        name="out_proj",
    )(x, merged, w_out, g_ffn)


def _top_rows(s, k):
    n = s.shape[0]
    row = lax.broadcasted_iota(jnp.int32, s.shape, 0).astype(F32)
    vals, idxs = [], []
    for _ in range(k):
        best = jnp.max(s, axis=0, keepdims=True)
        pick = jnp.min(jnp.where(s == best, row, float(n)), axis=0, keepdims=True)
        vals.append(best)
        idxs.append(pick)
        s = jnp.where(row == pick, -jnp.inf, s)
    return jnp.concatenate(vals, axis=0), jnp.concatenate(idxs, axis=0)


def _router_kernel(h_ref, wq_ref, sk_ref, a_ref, b_ref, g_ref):
    tm = h_ref.shape[0]
    k = PEER_TOPK
    q = jnp.dot(h_ref[...], wq_ref[...], preferred_element_type=F32).astype(BF16)
    half = PEER_DKEY // 2

    n_rows = k + 8 * SUBLANES
    r = lax.broadcasted_iota(jnp.int32, (n_rows, tm), 0)
    p_mid = ((r - k) >> 3) + 1
    q_mid = (r - k) & 7
    valid = (r < k) | (r >= k + 7 * SUBLANES) | ((p_mid + 1) * (q_mid + 1) <= k)
    r16 = lax.broadcasted_iota(jnp.int32, (k, tm), 0)

    def cand_rows(first, second):
        parts = [first[0:1] + second[0:k]]
        parts += [first[p:p + 1] + second[0:SUBLANES] for p in range(1, 8)]
        parts.append(first[8:16] + second[0:1])
        return jnp.concatenate(parts, axis=0)

    for h in range(PEER_HEADS):
        tops = []
        for m in range(2):
            c = 2 * h + m
            s = _nt_dot(sk_ref[m], q[:, c * half:(c + 1) * half])
            tops.append(_top_rows(s, k))
        (v1, i1), (v2, i2) = tops
        cand = jnp.where(valid, cand_rows(v1, v2), -jnp.inf)
        fv, frow = _top_rows(cand, k)
        frow = frow.astype(jnp.int32)
        tail = k + 7 * SUBLANES
        p_pos = jnp.where(frow < k, 0, jnp.where(frow >= tail, frow - tail + SUBLANES, ((frow - k) >> 3) + 1))
        q_pos = jnp.where(frow < k, frow, jnp.where(frow >= tail, 0, (frow - k) & 7))
        sel_a, sel_b = [], []
        for j in range(k):
            sel_a.append(jnp.max(jnp.where(r16 == p_pos[j:j + 1], i1, -1.0), axis=0, keepdims=True))
            sel_b.append(jnp.max(jnp.where(r16 == q_pos[j:j + 1], i2, -1.0), axis=0, keepdims=True))
        e = jnp.exp(fv - fv[0:1])
        gate = e / jnp.sum(e, axis=0, keepdims=True)
        a_ref[h * k:(h + 1) * k, :] = jnp.concatenate(sel_a, axis=0).astype(jnp.int32)
        b_ref[h * k:(h + 1) * k, :] = jnp.concatenate(sel_b, axis=0).astype(jnp.int32)
        g_ref[h * k:(h + 1) * k, :] = gate


def _router(hn, w_query, sub_keys, tm):
    n = hn.shape[0]
    slots = PEER_HEADS * PEER_TOPK
    out = pl.BlockSpec((slots, tm), lambda i: (0, i))
    return pl.pallas_call(
        _router_kernel,
        grid=(n // tm,),
        in_specs=[pl.BlockSpec((tm, D_MODEL), lambda i: (i, 0)),
                  pl.BlockSpec((D_MODEL, PEER_HEADS * PEER_DKEY), lambda i: (0, 0)),
                  pl.BlockSpec((2, PEER_KEYS, PEER_DKEY // 2), lambda i: (0, 0, 0))],
        out_specs=[out, out, out],
        out_shape=[jax.ShapeDtypeStruct((slots, n), jnp.int32),
                   jax.ShapeDtypeStruct((slots, n), jnp.int32),
                   jax.ShapeDtypeStruct((slots, n), F32)],
        compiler_params=_params(("parallel",)),
        name="router",
    )(hn, w_query, sub_keys)


def _wbuild_kernel(a_ref, b_ref, g_ref, w_ref):
    tm = a_ref.shape[0]
    idx = lax.broadcasted_iota(jnp.int32, (PEER_KEYS, a_ref.shape[1]), 0)
    grouped = (PEER_KEYS // SUBLANES, SUBLANES, PEER_KEYS)
    sub = lax.broadcasted_iota(jnp.int32, grouped, 1)

    def body(i, carry):
        base = pl.multiple_of(i * WBUILD_UNROLL, WBUILD_UNROLL)
        a_rows = a_ref[pl.ds(base, WBUILD_UNROLL), :]
        b_rows = b_ref[pl.ds(base, WBUILD_UNROLL), :]
        g_rows = g_ref[pl.ds(base, WBUILD_UNROLL), :]
        planes = []
        for t in range(WBUILD_UNROLL):
            first = jnp.where(a_rows[t:t + 1] == idx, g_rows[t:t + 1], 0.0).astype(BF16)
            second = jnp.where(b_rows[t:t + 1] == idx, 1.0, 0.0).astype(BF16)
            planes.append(_nt_dot(first, second).reshape(grouped))
        for d in (4, 2, 1):
            upper = (sub & d) != 0
            for t in range(WBUILD_UNROLL):
                if t & d == 0:
                    lo, hi = planes[t], planes[t + d]
                    planes[t] = jnp.where(upper, pltpu.roll(hi, d, axis=1), lo)
                    planes[t + d] = jnp.where(upper, hi, pltpu.roll(lo, SUBLANES - d, axis=1))
        for r in range(WBUILD_UNROLL):
            w_ref[i, :, r] = planes[r]
        return carry

    lax.fori_loop(0, tm // WBUILD_UNROLL, body, 0)


def _wbuild(a, b, g, tm):
    n, slots = a.shape
    row = pl.BlockSpec((tm, slots), lambda i: (i, 0))
    key_hi = PEER_KEYS // SUBLANES
    w = pl.pallas_call(
        _wbuild_kernel,
        grid=(n // tm,),
        in_specs=[row, row, row],
        out_specs=pl.BlockSpec((tm // WBUILD_UNROLL, key_hi, SUBLANES, WBUILD_UNROLL, PEER_KEYS),
                               lambda i: (i, 0, 0, 0, 0)),
        out_shape=jax.ShapeDtypeStruct((n // WBUILD_UNROLL, key_hi, SUBLANES, WBUILD_UNROLL, PEER_KEYS), F32),
        compiler_params=_params(("parallel",)),
        name="wbuild",
    )(a, b, g)
    return w.reshape(n // WBUILD_UNROLL, PEER_KEYS, WBUILD_UNROLL, PEER_KEYS)


def _experts_kernel(h_ref, ut_ref, v_ref, w_ref, o_ref):
    @pl.when(pl.program_id(1) == 0)
    def _():
        o_ref[...] = jnp.zeros_like(o_ref)

    tm = h_ref.shape[0]
    s = jnp.dot(h_ref[...], ut_ref[...], preferred_element_type=F32)
    w = jnp.concatenate([w_ref[:, r].reshape(tm, PEER_KEYS) for r in range(w_ref.shape[1])], axis=1)
    c = (jax.nn.gelu(s) * w).astype(BF16)
    o_ref[...] += jnp.dot(c, v_ref[...], preferred_element_type=F32)


def _experts(hn, u_t, v, w, tm):
    n = hn.shape[0]
    keys_per_tile = EXPERT_TILE // PEER_KEYS
    return pl.pallas_call(
        _experts_kernel,
        grid=(n // tm, PEER_EXPERTS // EXPERT_TILE),
        in_specs=[pl.BlockSpec((tm, D_MODEL), lambda i, j: (i, 0)),
                  pl.BlockSpec((D_MODEL, EXPERT_TILE), lambda i, j: (0, j)),
                  pl.BlockSpec((EXPERT_TILE, D_MODEL), lambda i, j: (j, 0)),
                  pl.BlockSpec((tm // WBUILD_UNROLL, keys_per_tile, WBUILD_UNROLL, PEER_KEYS),
                               lambda i, j: (i, j, 0, 0))],
        out_specs=pl.BlockSpec((tm, D_MODEL), lambda i, j: (i, 0)),
        out_shape=jax.ShapeDtypeStruct((n, D_MODEL), F32),
        compiler_params=_params(("parallel", "arbitrary")),
        name="experts",
    )(hn, u_t, v, w)


def _ple_kernel(x1_ref, peer_ref, p_ref, wp_ref, wg_ref, g_ref, o_ref):
    x2 = x1_ref[...] + peer_ref[...]
    ms = jnp.mean(x2 * x2, axis=-1, keepdims=True)
    hn = (x2 * lax.rsqrt(ms + RMS_EPS) * g_ref[...]).astype(BF16)
    emb = jnp.dot(p_ref[...].astype(BF16), wp_ref[...], preferred_element_type=F32)
    gate = jax.nn.sigmoid(jnp.dot(hn, wg_ref[...], preferred_element_type=F32))
    o_ref[...] = x2 + emb * gate


def _ple(x1, peer, p, w_ple, w_gate, g_ple, tm):
    n = x1.shape[0]
    row = pl.BlockSpec((tm, D_MODEL), lambda i: (i, 0))
    return pl.pallas_call(
        _ple_kernel,
        grid=(n // tm,),
        in_specs=[row, row, pl.BlockSpec((tm, PLE_DIM), lambda i: (i, 0)),
                  pl.BlockSpec((PLE_DIM, D_MODEL), lambda i: (0, 0)),
                  pl.BlockSpec((D_MODEL, D_MODEL), lambda i: (0, 0)),
                  pl.BlockSpec((1, D_MODEL), lambda i: (0, 0))],
        out_specs=row,
        out_shape=jax.ShapeDtypeStruct((n, D_MODEL), F32),
        compiler_params=_params(("parallel",)),
        name="ple",
    )(x1, peer, p, w_ple, w_gate, g_ple)


def _tile(n, pref):
    return pref if n % pref == 0 else n


def _layer(x, p, pos_rows, attention, ssm, wts):
    n = x.shape[0]
    z = _in_proj(x, wts["g_mix"], wts["w_in"], _tile(n, 1024), 1024)
    cos, sin_lo, sin_hi = _rope_tables(pos_rows)
    q, k_f32, k_bf, v_bf = _qkv_prep(z, wts["g_q"], wts["g_k"], wts["bd"], cos, sin_lo, sin_hi,
                                     _tile(cos.shape[0], 512))
    att = attention(q, k_bf, v_bf)
    ys, s_re, s_im = ssm(z)
    merged = _merge(att, ys, z, wts["w_attn_up"], wts["w_glu"], _tile(n, 512), 1024)
    x1, hn = _out_proj(x, merged, wts["w_out"], wts["g_ffn"], _tile(n, 512))
    a_t, b_t, g_t = _router(hn, wts["peer_w_query"], wts["peer_sub_keys"], 128)
    w = _wbuild(a_t.T, b_t.T, g_t.T, _tile(n, 128))
    peer = _experts(hn, wts["peer_u_t"], wts["peer_v"], w, _tile(n, 512))
    y = _ple(x1, peer, p, wts["w_ple"], wts["w_ple_gate"], wts["g_ple"], _tile(n, 512))
    k_new = k_f32.reshape(n, N_HEADS, 2 * HEAD_DK)
    v_new = z[:, 2 * QK_COLS:V_END].reshape(n, N_HEADS, HEAD_DV)
    return y, k_new, v_new, s_re, s_im


def kernel(x_prompt, x_sample, cache_k, cache_v, state_ssm_re, state_ssm_im, page_table, p_prompt, p_sample, g_mix, w_in, g_q, g_k, lambda_q, lambda_k, g_head, w_attn_up, ssm_a_re, ssm_a_im, ssm_log_dt, ssm_b_re, ssm_b_im, ssm_c_re, ssm_c_im, ssm_d, w_glu, w_out, g_ffn, peer_w_query, peer_sub_keys, peer_u, peer_v, g_ple, w_ple, w_ple_gate):
    depth = w_in.shape[0]
    assert depth == 1
    batch, seq, _ = x_prompt.shape
    n_dec, n_new, _ = x_sample.shape
    past_len = page_table.shape[1] * PAGE_SIZE
    i = 0
    lam_init = 0.8 - 0.6 * math.exp(-0.3 * i)

    row = lambda t: t.reshape(1, -1)
    bd = jnp.kron(jnp.eye(2, dtype=F32), jnp.ones((HEAD_DK, HEAD_DK), F32)).astype(BF16)
    wts = {
        "g_mix": row(g_mix[i]), "w_in": w_in[i].astype(BF16),
        "g_q": row(g_q[i]), "g_k": row(g_k[i]), "bd": bd,
        "w_attn_up": w_attn_up[i].astype(BF16), "w_glu": w_glu[i].astype(BF16),
        "w_out": w_out[i].astype(BF16), "g_ffn": row(g_ffn[i]),
        "peer_w_query": peer_w_query[i].astype(BF16), "peer_sub_keys": peer_sub_keys[i].astype(BF16),
        "peer_u_t": peer_u[i].T.astype(BF16), "peer_v": peer_v[i].astype(BF16),
        "g_ple": row(g_ple[i]), "w_ple": w_ple[i].astype(BF16), "w_ple_gate": w_ple_gate[i].astype(BF16),
    }
    g_head2 = row(g_head[i])
    lq, lk = lambda_q[i].astype(F32), lambda_k[i].astype(F32)
    lam = jnp.exp(jnp.sum(lq[0] * lk[0])) - jnp.exp(jnp.sum(lq[1] * lk[1])) + lam_init
    score_bound = (1.05 * QUERY_SCALE * HEAD_DK) * jnp.max(jnp.abs(g_q[i])) * jnp.max(jnp.abs(g_k[i]))
    lam = jnp.stack([lam, score_bound]).astype(F32)
    ssm_args = (ssm_a_re[i], ssm_a_im[i], ssm_log_dt[i], ssm_b_re[i], ssm_b_im[i],
                ssm_c_re[i], ssm_c_im[i], ssm_d[i])
    tables_prompt = tables_step = _ssm_tables(*ssm_args)

    pos_prompt = jnp.arange(seq, dtype=jnp.int32)
    y_p, k_p, v_p, sre_p, sim_p = _layer(
        x_prompt.reshape(batch * seq, D_MODEL), p_prompt[i].reshape(batch * seq, PLE_DIM), pos_prompt,
        lambda q, k, v: _flash_attention(q, k, v, g_head2, lam, lam_init, batch, seq, FLASH_BLOCK),
        lambda z: _ssm_prompt(z, tables_prompt, batch, seq),
        wts)

    pos_sample = jnp.tile(past_len + jnp.arange(n_new, dtype=jnp.int32), n_dec)
    y_s, k_s, v_s, sre_s, sim_s = _layer(
        x_sample.reshape(n_dec * n_new, D_MODEL), p_sample[i].reshape(n_dec * n_new, PLE_DIM), pos_sample,
        lambda q, k, v: _decode_attention(q, k, v, cache_k[i], cache_v[i], page_table, g_head2, lam,
                                          lam_init, n_dec, n_new),
        lambda z: _ssm_step(z, tables_step, state_ssm_re[i], state_ssm_im[i], n_dec, n_new),
        wts)

    lead = lambda t, *shape: t.reshape(1, *shape)
    return (y_p.reshape(batch, seq, D_MODEL), y_s.reshape(n_dec, n_new, D_MODEL),
            lead(k_p, batch, seq, N_HEADS, 2 * HEAD_DK), lead(v_p, batch, seq, N_HEADS, HEAD_DV),
            lead(sre_p, batch, SSM_GROUPS, SSM_STATE), lead(sim_p, batch, SSM_GROUPS, SSM_STATE),
            lead(k_s, n_dec, n_new, N_HEADS, 2 * HEAD_DK), lead(v_s, n_dec, n_new, N_HEADS, HEAD_DV),
            lead(sre_s, n_dec, SSM_GROUPS, SSM_STATE), lead(sim_s, n_dec, SSM_GROUPS, SSM_STATE))
```
